```python
import math
import jax, jax.numpy as jnp
from jax import lax
import numpy as np

D_MODEL = 1024
BATCH = 4
SEQ = 4096
DEPTH = 2

N_Q_HEADS = 8
N_KV_HEADS = 2
HEAD_DIM = 128
Q_GROUP = N_Q_HEADS // N_KV_HEADS
ATTN_WIDTH = N_Q_HEADS * HEAD_DIM
KV_WIDTH = N_KV_HEADS * HEAD_DIM
ROPE_AXIS_DIM = HEAD_DIM // 2
ROPE_THETA = 10000.0
GRID_W = 64
Q_BLOCK = 128

LRU_WIDTH = D_MODEL
LRU_BLOCKS = 8
LRU_BLOCK_W = LRU_WIDTH // LRU_BLOCKS
CONV_WIDTH = 4
CONV_PAD_LEFT = 2
CONV_PAD_RIGHT = 1
LRU_C = 8.0

N_BRANCHES = 2
IN_WIDTH = ATTN_WIDTH + 2 * KV_WIDTH + 2 * LRU_WIDTH + N_BRANCHES * D_MODEL

D_FF = 2816
N_EXPERTS = 8
TOP_K = 2
EXPERT_FF = 3584
N_DENSE = (DEPTH + 1) // 2
N_MOE = DEPTH // 2

N_MOD = 6
NORM_EPS = 1e-6

kernel_name = "hybrid_rglru_gqa_axialrope_moe_encoder"


def rms_norm(x, gain=None):
    xf = x.astype(jnp.float32)
    y = xf * lax.rsqrt(jnp.mean(xf * xf, axis=-1, keepdims=True) + NORM_EPS)
    if gain is not None:
        y = y * gain.astype(jnp.float32)
    return y.astype(x.dtype)


def modulate(x, shift, scale):
    return rms_norm(x) * (1 + scale) + shift


def axial_rope_angles(seq_len):
    rows = seq_len // GRID_W
    row = jnp.repeat(jnp.arange(rows, dtype=jnp.float32), GRID_W)
    col = jnp.tile(jnp.arange(GRID_W, dtype=jnp.float32), rows)
    n_freq = ROPE_AXIS_DIM // 2
    inv_freq = jnp.exp(-math.log(ROPE_THETA) * (2.0 * jnp.arange(n_freq, dtype=jnp.float32) / ROPE_AXIS_DIM))
    ang = jnp.stack([row[:, None] * inv_freq, col[:, None] * inv_freq], axis=1)
    return jnp.cos(ang), jnp.sin(ang)


def apply_axial_rope(x, cos, sin):
    xr = x.astype(jnp.float32).reshape(x.shape[:-1] + (2, 2, ROPE_AXIS_DIM // 2))
    x1 = xr[..., 0, :]
    x2 = xr[..., 1, :]
    c = cos[None, :, None]
    s = sin[None, :, None]
    out = jnp.stack([x1 * c - x2 * s, x2 * c + x1 * s], axis=-2)
    return out.reshape(x.shape).astype(x.dtype)


def gqa_attention(q, k, v):
    B, S = q.shape[0], q.shape[1]
    nqb = S // Q_BLOCK
    qb = q.reshape(B, nqb, Q_BLOCK, N_KV_HEADS, Q_GROUP, HEAD_DIM).transpose(1, 0, 2, 3, 4, 5)
    scale = HEAD_DIM ** -0.5

    def one_block(q_blk):
        s = jnp.einsum('bqkgd,bskd->bkgqs', q_blk, k, preferred_element_type=jnp.float32) * scale
        p = jax.nn.softmax(s, axis=-1)
        return jnp.einsum('bkgqs,bskd->bqkgd', p.astype(v.dtype), v)

    o = lax.map(one_block, qb)
    return o.transpose(1, 0, 2, 3, 4, 5).reshape(B, S, ATTN_WIDTH)


def centred_depthwise_conv(x, w, b):
    kern = w.astype(x.dtype)[:, None, :]
    y = lax.conv_general_dilated(x, kern, window_strides=(1,),
                                 padding=[(CONV_PAD_LEFT, CONV_PAD_RIGHT)],
                                 dimension_numbers=('NWC', 'WIO', 'NWC'),
                                 feature_group_count=x.shape[-1])
    return y + b.astype(x.dtype)


def block_diag_linear(x, w, b):
    B, S, _ = x.shape
    xb = x.reshape(B, S, LRU_BLOCKS, LRU_BLOCK_W)
    y = jnp.einsum('bsni,nij->bsnj', xb, w.astype(jnp.float32)).reshape(B, S, LRU_WIDTH)
    return y + b.astype(jnp.float32)


def rg_lru(x, w_a, b_a, w_x, b_x, lam, reverse):
    xf = x.astype(jnp.float32)
    r = jax.nn.sigmoid(block_diag_linear(xf, w_a, b_a))
    i = jax.nn.sigmoid(block_diag_linear(xf, w_x, b_x))
    log_a = -LRU_C * r * jax.nn.softplus(-lam.astype(jnp.float32))
    a = jnp.exp(log_a)
    u = jnp.sqrt(-jnp.expm1(2.0 * log_a)) * (i * xf)

    def step(h, au):
        a_t, u_t = au
        h = a_t * h + u_t
        return h, h

    h0 = jnp.zeros((x.shape[0], LRU_WIDTH), jnp.float32)
    _, hs = lax.scan(step, h0, (a.transpose(1, 0, 2), u.transpose(1, 0, 2)), reverse=reverse)
    return hs.transpose(1, 0, 2)


def swiglu(h, w_gate, w_up, w_down):
    return (jax.nn.silu(h @ w_gate) * (h @ w_up)) @ w_down


def moe_swiglu(h, router_w, router_b, w_gate, w_up, w_down):
    B, S, D = h.shape
    t = h.reshape(B * S, D)
    logits = (t @ router_w).astype(jnp.float32) + router_b.astype(jnp.float32)
    top_vals, top_idx = lax.top_k(logits, TOP_K)
    top_w = jax.nn.softmax(top_vals, axis=-1)
    combine = jnp.sum(jax.nn.one_hot(top_idx, N_EXPERTS, dtype=jnp.float32) * top_w[..., None], axis=1)
    out = jnp.zeros((B * S, D), jnp.float32)
    for e in range(N_EXPERTS):
        y = swiglu(t, w_gate[e], w_up[e], w_down[e])
        out = out + combine[:, e:e + 1] * y.astype(jnp.float32)
    return out.astype(h.dtype).reshape(B, S, D)


def hybrid_mixer(h, cos, sin, w_in, q_gain, k_gain, conv_w, conv_b,
                 lru_w_a, lru_b_a, lru_w_x, lru_b_x, lru_lambda,
                 w_o_attn, w_o_lru, w_out):
    B, S, _ = h.shape
    proj = h @ w_in
    cuts = np.cumsum([ATTN_WIDTH, KV_WIDTH, KV_WIDTH, LRU_WIDTH, LRU_WIDTH]).tolist()
    q, k, v, xr, xg, gates = jnp.split(proj, cuts, axis=-1)

    q = rms_norm(q.reshape(B, S, N_Q_HEADS, HEAD_DIM), q_gain)
    k = rms_norm(k.reshape(B, S, N_KV_HEADS, HEAD_DIM), k_gain)
    v = v.reshape(B, S, N_KV_HEADS, HEAD_DIM)
    q = apply_axial_rope(q, cos, sin)
    k = apply_axial_rope(k, cos, sin)
    y_attn = gqa_attention(q, k, v) @ w_o_attn

    xc = centred_depthwise_conv(xr, conv_w, conv_b)
    h_fwd = rg_lru(xc, lru_w_a[0], lru_b_a[0], lru_w_x[0], lru_b_x[0], lru_lambda[0], reverse=False)
    h_bwd = rg_lru(xc, lru_w_a[1], lru_b_a[1], lru_w_x[1], lru_b_x[1], lru_lambda[1], reverse=True)
    y_lru = ((h_fwd + h_bwd).astype(h.dtype) * jax.nn.gelu(xg, approximate=True)) @ w_o_lru

    g_attn, g_lru = jnp.split(jax.nn.sigmoid(gates), N_BRANCHES, axis=-1)
    merged = g_attn * y_attn + g_lru * y_lru
    return merged @ w_out


def setup_inputs(seed: int = 0) -> dict:
    key = jax.random.key(seed)
    ks = iter(jax.random.split(key, 40))
    f32 = jnp.float32

    def nrm(shape, fan_in, mult=1.0):
        return jax.random.normal(next(ks), shape, f32) * (mult * fan_in ** -0.5)

    def small(shape, s=0.01):
        return jax.random.normal(next(ks), shape, f32) * s

    def gain(shape):
        return 1.0 + jax.random.normal(next(ks), shape, f32) * 0.02

    x = jax.random.normal(next(ks), (BATCH, SEQ, D_MODEL), f32)
    c = jax.random.normal(next(ks), (BATCH, D_MODEL), f32)
    w_mod = nrm((DEPTH, D_MODEL, N_MOD * D_MODEL), D_MODEL, 0.5)
    b_mod = small((DEPTH, N_MOD * D_MODEL))
    w_in = nrm((DEPTH, D_MODEL, IN_WIDTH), D_MODEL)
    q_norm_gain = gain((DEPTH, HEAD_DIM))
    k_norm_gain = gain((DEPTH, HEAD_DIM))
    conv_w = nrm((DEPTH, CONV_WIDTH, LRU_WIDTH), CONV_WIDTH)
    conv_b = small((DEPTH, LRU_WIDTH))
    lru_w_a = nrm((DEPTH, 2, LRU_BLOCKS, LRU_BLOCK_W, LRU_BLOCK_W), LRU_BLOCK_W)
    lru_b_a = small((DEPTH, 2, LRU_WIDTH))
    lru_w_x = nrm((DEPTH, 2, LRU_BLOCKS, LRU_BLOCK_W, LRU_BLOCK_W), LRU_BLOCK_W)
    lru_b_x = small((DEPTH, 2, LRU_WIDTH))
    a0 = jax.random.uniform(next(ks), (DEPTH, 2, LRU_WIDTH), f32, 0.9, 0.999)
    s0 = a0 ** (1.0 / LRU_C)
    lru_lambda = jnp.log(s0) - jnp.log1p(-s0)
    w_o_attn = nrm((DEPTH, ATTN_WIDTH, D_MODEL), ATTN_WIDTH)
    w_o_lru = nrm((DEPTH, LRU_WIDTH, D_MODEL), LRU_WIDTH)
    w_out = nrm((DEPTH, D_MODEL, D_MODEL), D_MODEL)
    ffn_w_gate = nrm((N_DENSE, D_MODEL, D_FF), D_MODEL)
    ffn_w_up = nrm((N_DENSE, D_MODEL, D_FF), D_MODEL)
    ffn_w_down = nrm((N_DENSE, D_FF, D_MODEL), D_FF)
    router_w = nrm((N_MOE, D_MODEL, N_EXPERTS), D_MODEL)
    router_b = small((N_MOE, N_EXPERTS))
    moe_w_gate = nrm((N_MOE, N_EXPERTS, D_MODEL, EXPERT_FF), D_MODEL)
    moe_w_up = nrm((N_MOE, N_EXPERTS, D_MODEL, EXPERT_FF), D_MODEL)
    moe_w_down = nrm((N_MOE, N_EXPERTS, EXPERT_FF, D_MODEL), EXPERT_FF)
    final_gain = gain((D_MODEL,))
    return {"x": x, "c": c, "w_mod": w_mod, "b_mod": b_mod, "w_in": w_in,
            "q_norm_gain": q_norm_gain, "k_norm_gain": k_norm_gain,
            "conv_w": conv_w, "conv_b": conv_b,
            "lru_w_a": lru_w_a, "lru_b_a": lru_b_a, "lru_w_x": lru_w_x, "lru_b_x": lru_b_x,
            "lru_lambda": lru_lambda, "w_o_attn": w_o_attn, "w_o_lru": w_o_lru, "w_out": w_out,
            "ffn_w_gate": ffn_w_gate, "ffn_w_up": ffn_w_up, "ffn_w_down": ffn_w_down,
            "router_w": router_w, "router_b": router_b,
            "moe_w_gate": moe_w_gate, "moe_w_up": moe_w_up, "moe_w_down": moe_w_down,
            "final_gain": final_gain}


def reference(x, c, w_mod, b_mod, w_in, q_norm_gain, k_norm_gain, conv_w, conv_b,
              lru_w_a, lru_b_a, lru_w_x, lru_b_x, lru_lambda, w_o_attn, w_o_lru, w_out,
              ffn_w_gate, ffn_w_up, ffn_w_down, router_w, router_b,
              moe_w_gate, moe_w_up, moe_w_down, final_gain):
    cos, sin = axial_rope_angles(x.shape[1])
    c_act = jax.nn.silu(c)
    for l in range(DEPTH):
        mod = (c_act @ w_mod[l] + b_mod[l])[:, None, :]
        sh1, sc1, g1, sh2, sc2, g2 = jnp.split(mod, N_MOD, axis=-1)

        h = modulate(x, sh1, sc1)
        mix = hybrid_mixer(h, cos, sin, w_in[l], q_norm_gain[l], k_norm_gain[l],
                           conv_w[l], conv_b[l], lru_w_a[l], lru_b_a[l], lru_w_x[l], lru_b_x[l],
                           lru_lambda[l], w_o_attn[l], w_o_lru[l], w_out[l])
        x = x + g1 * mix

        h = modulate(x, sh2, sc2)
        if l % 2 == 0:
            j = l // 2
            ff = swiglu(h, ffn_w_gate[j], ffn_w_up[j], ffn_w_down[j])
        else:
            j = l // 2
            ff = moe_swiglu(h, router_w[j], router_b[j], moe_w_gate[j], moe_w_up[j], moe_w_down[j])
        x = x + g2 * ff
    return rms_norm(x, final_gain)
```

```python
import functools
import math

import jax
import jax.numpy as jnp
from jax import lax
from jax.experimental import pallas as pl
from jax.experimental.pallas import tpu as pltpu

HEAD_DIM = 128
N_Q_HEADS = 8
N_KV_HEADS = 2
Q_GROUP = N_Q_HEADS // N_KV_HEADS
GRID_W = 64
ROPE_THETA = 10000.0
ROPE_PAIR_SHIFT = HEAD_DIM // 4
LRU_BLOCKS = 8
LRU_C = 8.0
CONV_WIDTH = 4
N_MOD = 6
TOP_K = 2
NORM_EPS = 1e-6

LANES = 128
SUBLANES = 8
VMEM_LIMIT_BYTES = 56 * 1024 * 1024

TM_INPROJ = 256
TM_LRU = 256
TQ_ATTN = 128
TM_MERGE = 512
TM_FFN = 256
TM_ROUTER = 256
TM_EXPERT = 256
TM_SCATTER = 256
TM_COMBINE = 256
FF_CHUNK = 512

BF16 = jnp.bfloat16
F32 = jnp.float32


def _params(*semantics):
    return pltpu.CompilerParams(dimension_semantics=semantics, vmem_limit_bytes=VMEM_LIMIT_BYTES)


def _resident(shape):
    zeros = (0,) * len(shape)
    return pl.BlockSpec(shape, lambda *_: zeros)


def _modulated_norm(x, shift, scale):
    ms = jnp.mean(x * x, axis=-1, keepdims=True)
    return x * lax.rsqrt(ms + NORM_EPS) * (1.0 + scale) + shift


def _mod_kernel(c_ref, w_ref, b_ref, o_ref):
    c = c_ref[...]
    act = c * jax.nn.sigmoid(c)
    o_ref[...] = jnp.dot(act, w_ref[...], preferred_element_type=F32,
                         precision=lax.Precision.HIGHEST) + b_ref[...]


def _modulation(c, w_mod, b_mod):
    depth, d, n = w_mod.shape
    b = c.shape[0]
    tn = 1536
    return pl.pallas_call(
        _mod_kernel,
        grid=(depth, n // tn),
        in_specs=[pl.BlockSpec((b, d), lambda l, j: (0, 0)),
                  pl.BlockSpec((None, d, tn), lambda l, j: (l, 0, j)),
                  pl.BlockSpec((None, 1, tn), lambda l, j: (l, 0, j))],
        out_specs=pl.BlockSpec((None, b, tn), lambda l, j: (l, 0, j)),
        out_shape=jax.ShapeDtypeStruct((depth, b, n), F32),
        compiler_params=_params("parallel", "parallel"),
        name="modulation",
    )(c, w_mod, b_mod.reshape(depth, 1, n))


def _inproj_kernel(x_ref, mod_ref, w_ref, qg_ref, kg_ref, cos_ref, sin_ref,
                   q_ref, k_ref, v_ref, xr_ref, gxg_ref, sg_ref, *, d_model):
    attn_w = N_Q_HEADS * HEAD_DIM
    kv_w = N_KV_HEADS * HEAD_DIM
    mod = mod_ref[...]
    hb = _modulated_norm(x_ref[...], mod[0:1], mod[1:2]).astype(BF16)

    cos = cos_ref[...]
    sin = sin_ref[...]
    lane = lax.broadcasted_iota(jnp.int32, cos.shape, 1)
    pair_first = (lane & ROPE_PAIR_SHIFT) == 0

    def norm_rope(y, gain, post_scale):
        r = lax.rsqrt(jnp.mean(y * y, axis=-1, keepdims=True) + NORM_EPS)
        yn = y * r * gain
        partner = jnp.where(pair_first,
                            pltpu.roll(yn, HEAD_DIM - ROPE_PAIR_SHIFT, 1),
                            pltpu.roll(yn, ROPE_PAIR_SHIFT, 1))
        return (yn * cos + partner * sin) * post_scale

    def proj(lo, width):
        return jnp.dot(hb, w_ref[:, lo:lo + width], preferred_element_type=F32)

    qf = proj(0, attn_w)
    qg = qg_ref[...]
    for h in range(N_Q_HEADS):
        sl = slice(h * HEAD_DIM, (h + 1) * HEAD_DIM)
        q_ref[:, sl] = norm_rope(qf[:, sl], qg, HEAD_DIM ** -0.5).astype(BF16)
    kf = proj(attn_w, kv_w)
    kg = kg_ref[...]
    for h in range(N_KV_HEADS):
        sl = slice(h * HEAD_DIM, (h + 1) * HEAD_DIM)
        k_ref[:, sl] = norm_rope(kf[:, sl], kg, 1.0).astype(BF16)
    v_ref[...] = proj(attn_w + kv_w, kv_w).astype(BF16)
    lo = attn_w + 2 * kv_w
    xr_ref[...] = proj(lo, d_model)
    gxg_ref[...] = jax.nn.gelu(proj(lo + d_model, d_model), approximate=True).astype(BF16)
    sg_ref[...] = jax.nn.sigmoid(proj(lo + 2 * d_model, 2 * d_model)).astype(BF16)


def _inproj(x, mod, w_in, q_gain, k_gain, cos, sin):
    b, s, d = x.shape
    tm = min(TM_INPROJ, s)
    attn_w = N_Q_HEADS * HEAD_DIM
    kv_w = N_KV_HEADS * HEAD_DIM
    row = lambda w: pl.BlockSpec((None, tm, w), lambda bi, i: (bi, i, 0))
    tab = pl.BlockSpec((tm, HEAD_DIM), lambda bi, i: (i, 0))
    outs = [(attn_w, BF16), (kv_w, BF16), (kv_w, BF16), (d, F32), (d, BF16), (2 * d, BF16)]
    return pl.pallas_call(
        functools.partial(_inproj_kernel, d_model=d),
        grid=(b, s // tm),
        in_specs=[row(d),
                  pl.BlockSpec((None, N_MOD, d), lambda bi, i: (bi, 0, 0)),
                  _resident(w_in.shape), _resident((1, HEAD_DIM)), _resident((1, HEAD_DIM)),
                  tab, tab],
        out_specs=[row(w) for w, _ in outs],
        out_shape=[jax.ShapeDtypeStruct((b, s, w), dt) for w, dt in outs],
        compiler_params=_params("parallel", "parallel"),
        name="inproj",
    )(x, mod, w_in, q_gain.reshape(1, HEAD_DIM), k_gain.reshape(1, HEAD_DIM), cos, sin)


def _lru_kernel(*refs, ts, n_tiles, reverse, final):
    if final:
        (xr_ref, prev_ref, next_ref, cw_ref, cb_ref, wcat_ref, ba_ref, bx_ref, lam_ref,
         hb_ref, gxg_ref, o_ref, a_scr, u_scr, carry_scr) = refs
    else:
        (xr_ref, prev_ref, next_ref, cw_ref, cb_ref, wcat_ref, ba_ref, bx_ref, lam_ref,
         o_ref, a_scr, u_scr, carry_scr) = refs
    j = pl.program_id(1)
    tile = (n_tiles - 1 - j) if reverse else j
    bw = LANES

    @pl.when(j == 0)
    def _():
        carry_scr[...] = jnp.zeros_like(carry_scr)

    prev = jnp.where(tile > 0, prev_ref[...], 0.0)
    nxt = jnp.where(tile < n_tiles - 1, next_ref[...], 0.0)
    xfull = jnp.concatenate([prev, xr_ref[...], nxt], axis=0)
    rows = ts + 2 * SUBLANES
    cw = cw_ref[...]

    def tap(shift):
        rolled = xfull if shift == 0 else pltpu.roll(xfull, (-shift) % rows, 0)
        return rolled[SUBLANES:SUBLANES + ts]

    xc = (cw[0:1] * tap(-2) + cw[1:2] * tap(-1) + cw[2:3] * tap(0) + cw[3:4] * tap(1)
          + cb_ref[...])

    lam = lam_ref[...]
    neg_c_softplus = -LRU_C * (jnp.maximum(-lam, 0.0) + jnp.log1p(jnp.exp(-jnp.abs(lam))))
    ba = ba_ref[...]
    bx = bx_ref[...]
    for n in range(LRU_BLOCKS):
        sl = slice(n * bw, (n + 1) * bw)
        xb = xc[:, sl]
        g = jnp.dot(xb.astype(BF16), wcat_ref[n], preferred_element_type=F32)
        r = jax.nn.sigmoid(g[:, :bw] + ba[:, sl])
        i = jax.nn.sigmoid(g[:, bw:] + bx[:, sl])
        log_a = neg_c_softplus[:, sl] * r
        a_scr[:, sl] = jnp.exp(log_a)
        th = jnp.tanh(log_a)
        u_scr[:, sl] = jnp.sqrt(-2.0 * th / (1.0 - th)) * (i * xb)

    def step(t, h):
        tt = (ts - 1 - t) if reverse else t
        h = a_scr[pl.ds(tt, 1), :] * h + u_scr[pl.ds(tt, 1), :]
        u_scr[pl.ds(tt, 1), :] = h
        return h

    carry_scr[0:1, :] = lax.fori_loop(0, ts, step, carry_scr[0:1, :], unroll=8)
    if final:
        o_ref[...] = ((u_scr[...] + hb_ref[...]) * gxg_ref[...].astype(F32)).astype(BF16)
    else:
        o_ref[...] = u_scr[...]


def _lru_pass(xr, conv_w, conv_b, wcat, b_a, b_x, lam, reverse, h_other=None, gxg=None):
    b, s, w = xr.shape
    ts = min(TM_LRU, s)
    n_tiles = s // ts
    blocks8 = ts // SUBLANES
    final = h_other is not None
    tile = (lambda j: n_tiles - 1 - j) if reverse else (lambda j: j)
    main = pl.BlockSpec((None, ts, w), lambda bi, j: (bi, tile(j), 0))
    prev = pl.BlockSpec((None, SUBLANES, w),
                        lambda bi, j: (bi, jnp.maximum(tile(j) * blocks8 - 1, 0), 0))
    nxt = pl.BlockSpec((None, SUBLANES, w),
                       lambda bi, j: (bi, jnp.minimum((tile(j) + 1) * blocks8, s // SUBLANES - 1), 0))
    vec = _resident((1, w))
    in_specs = [main, prev, nxt, _resident((CONV_WIDTH, w)), vec, _resident(wcat.shape), vec, vec, vec]
    args = [xr, xr, xr, conv_w, conv_b.reshape(1, w), wcat, b_a.reshape(1, w), b_x.reshape(1, w),
            lam.reshape(1, w)]
    if final:
        in_specs += [main, main]
        args += [h_other, gxg]
    return pl.pallas_call(
        functools.partial(_lru_kernel, ts=ts, n_tiles=n_tiles, reverse=reverse, final=final),
        grid=(b, n_tiles),
        in_specs=in_specs,
        out_specs=main,
        out_shape=jax.ShapeDtypeStruct((b, s, w), BF16 if final else F32),
        scratch_shapes=[pltpu.VMEM((ts, w), F32), pltpu.VMEM((ts, w), F32),
                        pltpu.VMEM((SUBLANES, w), F32)],
        compiler_params=_params("parallel", "arbitrary"),
        name="lru_fwd" if final else "lru_bwd",
    )(*args)


def _lru_branch(xr, gxg, conv_w, conv_b, w_a, b_a, w_x, b_x, lam):
    wcat = jnp.concatenate([w_a, w_x], axis=-1).astype(BF16)
    h_bwd = _lru_pass(xr, conv_w, conv_b, wcat[1], b_a[1], b_x[1], lam[1], reverse=True)
    return _lru_pass(xr, conv_w, conv_b, wcat[0], b_a[0], b_x[0], lam[0], reverse=False,
                     h_other=h_bwd, gxg=gxg)


def _attn_kernel(q_ref, k_ref, v_ref, o_ref, *, tq):
    q = q_ref[...]
    qs = jnp.concatenate([q[:, h * HEAD_DIM:(h + 1) * HEAD_DIM] for h in range(Q_GROUP)], axis=0)
    s = lax.dot_general(qs, k_ref[...], (((1,), (1,)), ((), ())), preferred_element_type=F32)
    m = jnp.max(s, axis=-1, keepdims=True)
    p = jnp.exp(s - m)
    l = jnp.sum(p, axis=-1, keepdims=True)
    o = jnp.dot(p.astype(BF16), v_ref[...], preferred_element_type=F32) * (1.0 / l)
    for h in range(Q_GROUP):
        o_ref[:, h * HEAD_DIM:(h + 1) * HEAD_DIM] = o[h * tq:(h + 1) * tq].astype(BF16)


def _attention(q, k, v):
    b, s, _ = q.shape
    tq = min(TQ_ATTN, s)
    gw = Q_GROUP * HEAD_DIM
    qspec = pl.BlockSpec((None, tq, gw), lambda bi, g, i: (bi, i, g))
    kvspec = pl.BlockSpec((None, s, HEAD_DIM), lambda bi, g, i: (bi, 0, g))
    return pl.pallas_call(
        functools.partial(_attn_kernel, tq=tq),
        grid=(b, N_KV_HEADS, s // tq),
        in_specs=[qspec, kvspec, kvspec],
        out_specs=qspec,
        out_shape=jax.ShapeDtypeStruct(q.shape, BF16),
        compiler_params=_params("parallel", "parallel", "arbitrary"),
        name="attention",
    )(q, k, v)


def _merge_kernel(attn_ref, ylru_ref, sg_ref, x_ref, mod_ref, woa_ref, wol_ref, wout_ref, o_ref,
                  *, d_model):
    ya = jnp.dot(attn_ref[...], woa_ref[...], preferred_element_type=F32)
    yl = jnp.dot(ylru_ref[...], wol_ref[...], preferred_element_type=F32)
    sg = sg_ref[...].astype(F32)
    merged = sg[:, :d_model] * ya + sg[:, d_model:] * yl
    out = jnp.dot(merged.astype(BF16), wout_ref[...], preferred_element_type=F32)
    o_ref[...] = x_ref[...] + mod_ref[...][2:3] * out


def _merge(attn, ylru, sg, x, mod, w_o_attn, w_o_lru, w_out):
    b, s, d = x.shape
    tm = min(TM_MERGE, s)
    row = lambda w: pl.BlockSpec((None, tm, w), lambda bi, i: (bi, i, 0))
    return pl.pallas_call(
        functools.partial(_merge_kernel, d_model=d),
        grid=(b, s // tm),
        in_specs=[row(attn.shape[-1]), row(d), row(2 * d), row(d),
                  pl.BlockSpec((None, N_MOD, d), lambda bi, i: (bi, 0, 0)),
                  _resident(w_o_attn.shape), _resident(w_o_lru.shape), _resident(w_out.shape)],
        out_specs=row(d),
        out_shape=jax.ShapeDtypeStruct(x.shape, F32),
        compiler_params=_params("parallel", "parallel"),
        name="merge",
    )(attn, ylru, sg, x, mod, w_o_attn, w_o_lru, w_out)


def _swiglu_chunks(hb, wg_ref, wu_ref, wd_ref):
    d_ff = wg_ref.shape[-1]
    acc = None
    for lo in range(0, d_ff, FF_CHUNK):
        hi = min(lo + FF_CHUNK, d_ff)
        g = jnp.dot(hb, wg_ref[:, lo:hi], preferred_element_type=F32)
        u = jnp.dot(hb, wu_ref[:, lo:hi], preferred_element_type=F32)
        act = (g * jax.nn.sigmoid(g) * u).astype(BF16)
        y = jnp.dot(act, wd_ref[lo:hi, :], preferred_element_type=F32)
        acc = y if acc is None else acc + y
    return acc


def _final_norm(x, gain):
    return x * lax.rsqrt(jnp.mean(x * x, axis=-1, keepdims=True) + NORM_EPS) * gain


def _ffn_kernel(x_ref, mod_ref, wg_ref, wu_ref, wd_ref, fg_ref, o_ref, *, final):
    x = x_ref[...]
    mod = mod_ref[...]
    hb = _modulated_norm(x, mod[3:4], mod[4:5]).astype(BF16)
    y = x + mod[5:6] * _swiglu_chunks(hb, wg_ref, wu_ref, wd_ref)
    o_ref[...] = _final_norm(y, fg_ref[...]) if final else y


def _dense_ffn(x, mod, w_gate, w_up, w_down, final_gain, final):
    b, s, d = x.shape
    tm = min(TM_FFN, s)
    row = pl.BlockSpec((None, tm, d), lambda bi, i: (bi, i, 0))
    return pl.pallas_call(
        functools.partial(_ffn_kernel, final=final),
        grid=(b, s // tm),
        in_specs=[row, pl.BlockSpec((None, N_MOD, d), lambda bi, i: (bi, 0, 0)),
                  _resident(w_gate.shape), _resident(w_up.shape), _resident(w_down.shape),
                  _resident((1, d))],
        out_specs=row,
        out_shape=jax.ShapeDtypeStruct(x.shape, F32),
        compiler_params=_params("parallel", "parallel"),
        name="dense_ffn",
    )(x, mod, w_gate, w_up, w_down, final_gain.reshape(1, d))


R_E0, R_E1, R_RANK0, R_RANK1, R_W0, R_W1 = range(6)


def _router_kernel(x_ref, mod_ref, rw_ref, rb_ref, hp_ref, info_ref, cnt_ref, carry_scr, *, tm):
    @pl.when((pl.program_id(0) == 0) & (pl.program_id(1) == 0))
    def _():
        carry_scr[...] = jnp.zeros_like(carry_scr)

    mod = mod_ref[...]
    h = _modulated_norm(x_ref[...], mod[3:4], mod[4:5])
    hp_ref[...] = h

    logits = jnp.dot(h, rw_ref[...], preferred_element_type=F32,
                     precision=lax.Precision.HIGHEST) + rb_ref[...]
    lane = lax.broadcasted_iota(jnp.int32, logits.shape, 1)
    m1 = jnp.max(logits, axis=-1, keepdims=True)
    e0 = jnp.min(jnp.where(logits == m1, lane, LANES), axis=-1, keepdims=True)
    rest = jnp.where(lane == e0, -jnp.inf, logits)
    m2 = jnp.max(rest, axis=-1, keepdims=True)
    e1 = jnp.min(jnp.where(rest == m2, lane, LANES), axis=-1, keepdims=True)
    t = jnp.exp(m2 - m1)
    w0 = 1.0 / (1.0 + t)
    w1 = t * w0

    hit0 = lane == e0
    hit1 = lane == e1
    onehot = jnp.where(hit0 | hit1, 1.0, 0.0)
    ri = lax.broadcasted_iota(jnp.int32, (tm, tm), 0)
    ci = lax.broadcasted_iota(jnp.int32, (tm, tm), 1)
    before = jnp.where(ci < ri, 1.0, 0.0).astype(BF16)
    seen = jnp.dot(before, onehot.astype(BF16), preferred_element_type=F32) + carry_scr[0:1, :]
    rank0 = jnp.sum(jnp.where(hit0, seen, 0.0), axis=-1, keepdims=True)
    rank1 = jnp.sum(jnp.where(hit1, seen, 0.0), axis=-1, keepdims=True)
    carry = carry_scr[0:1, :] + jnp.sum(onehot, axis=0, keepdims=True)
    carry_scr[0:1, :] = carry
    cnt_ref[...] = jnp.broadcast_to(carry, cnt_ref.shape)

    info = jnp.zeros(logits.shape, F32)
    for col, val in ((R_E0, e0.astype(F32)), (R_E1, e1.astype(F32)), (R_RANK0, rank0),
                     (R_RANK1, rank1), (R_W0, w0), (R_W1, w1)):
        info = jnp.where(lane == col, val, info)
    info_ref[...] = info


def _router(x, mod, router_w, router_b):
    b, s, d = x.shape
    tm = min(TM_ROUTER, s)
    n_e = router_w.shape[-1]
    rw = jnp.zeros((d, LANES), F32).at[:, :n_e].set(router_w)
    rb = jnp.full((1, LANES), -jnp.inf, F32).at[0, :n_e].set(router_b)
    nt = s // tm
    return pl.pallas_call(
        functools.partial(_router_kernel, tm=tm),
        grid=(b, nt),
        in_specs=[pl.BlockSpec((None, tm, d), lambda bi, i: (bi, i, 0)),
                  pl.BlockSpec((None, N_MOD, d), lambda bi, i: (bi, 0, 0)),
                  _resident(rw.shape), _resident(rb.shape)],
        out_specs=[pl.BlockSpec((tm, d), lambda bi, i: (bi * nt + i, 0)),
                   pl.BlockSpec((tm, LANES), lambda bi, i: (bi * nt + i, 0)),
                   _resident((SUBLANES, LANES))],
        out_shape=[jax.ShapeDtypeStruct((b * s, d), F32),
                   jax.ShapeDtypeStruct((b * s, LANES), F32),
                   jax.ShapeDtypeStruct((SUBLANES, LANES), F32)],
        scratch_shapes=[pltpu.VMEM((SUBLANES, LANES), F32)],
        compiler_params=_params("arbitrary", "arbitrary"),
        name="router",
    )(x, mod, rw, rb)


def _scatter_kernel(pos_ref, hp_ref, init_ref, xs_ref, sem, *, tm):
    del init_ref

    def copy(r, k):
        return pltpu.make_async_copy(hp_ref.at[pl.ds(r, 1)],
                                     xs_ref.at[pl.ds(pos_ref[0, 0, TOP_K * r + k], 1)], sem)

    def start(r, carry):
        for k in range(TOP_K):
            copy(r, k).start()
        return carry

    def wait(r, carry):
        for k in range(TOP_K):
            copy(r, k).wait()
        return carry

    lax.fori_loop(0, tm, start, 0, unroll=8)
    lax.fori_loop(0, tm, wait, 0, unroll=8)


def _scatter_rows(hp, pos, n_rows):
    n, w = hp.shape
    tm = min(TM_SCATTER, n)
    pos3 = pos.reshape(n // tm, 1, TOP_K * tm)
    return pl.pallas_call(
        functools.partial(_scatter_kernel, tm=tm),
        grid=(n // tm,),
        in_specs=[pl.BlockSpec((1, 1, TOP_K * tm), lambda i: (i, 0, 0), memory_space=pltpu.SMEM),
                  pl.BlockSpec((tm, w), lambda i: (i, 0)),
                  pl.BlockSpec(memory_space=pl.ANY)],
        out_specs=pl.BlockSpec(memory_space=pl.ANY),
        out_shape=jax.ShapeDtypeStruct((n_rows, w), hp.dtype),
        scratch_shapes=[pltpu.SemaphoreType.DMA(())],
        input_output_aliases={2: 0},
        compiler_params=_params("arbitrary"),
        name="moe_scatter",
    )(pos3, hp, jnp.zeros((n_rows, w), hp.dtype))


def _expert_kernel(te_ref, nu_ref, xs_ref, wg_ref, wu_ref, wd_ref, y_ref):
    del te_ref
    i = pl.program_id(0)

    @pl.when(i < nu_ref[0])
    def _():
        y_ref[...] = _swiglu_chunks(xs_ref[...].astype(BF16), wg_ref, wu_ref, wd_ref)

    @pl.when(i >= nu_ref[0])
    def _():
        y_ref[...] = jnp.zeros_like(y_ref)


def _grouped_experts(xs, tile_expert, n_used, w_gate, w_up, w_down):
    n_rows, d = xs.shape
    tm = TM_EXPERT
    ff = w_gate.shape[-1]
    grid_spec = pltpu.PrefetchScalarGridSpec(
        num_scalar_prefetch=2,
        grid=(n_rows // tm,),
        in_specs=[pl.BlockSpec((tm, d), lambda i, te, nu: (i, 0)),
                  pl.BlockSpec((None, d, ff), lambda i, te, nu: (te[i], 0, 0)),
                  pl.BlockSpec((None, d, ff), lambda i, te, nu: (te[i], 0, 0)),
                  pl.BlockSpec((None, ff, d), lambda i, te, nu: (te[i], 0, 0))],
        out_specs=pl.BlockSpec((tm, d), lambda i, te, nu: (i, 0)),
    )
    return pl.pallas_call(
        _expert_kernel,
        grid_spec=grid_spec,
        out_shape=jax.ShapeDtypeStruct((n_rows, d), F32),
        compiler_params=pltpu.CompilerParams(dimension_semantics=("arbitrary",),
                                             vmem_limit_bytes=60 * 1024 * 1024),
        name="moe_experts",
    )(tile_expert, n_used, xs, w_gate, w_up, w_down)


def _combine_kernel(pos_ref, y_ref, x_ref, mod_ref, info_ref, fg_ref, o_ref, buf, sem, *, tm, final):
    def copy(r, k):
        return pltpu.make_async_copy(y_ref.at[pl.ds(pos_ref[0, 0, TOP_K * r + k], 1)],
                                     buf.at[k, pl.ds(r, 1)], sem)

    def start(r, carry):
        for k in range(TOP_K):
            copy(r, k).start()
        return carry

    def wait(r, carry):
        for k in range(TOP_K):
            copy(r, k).wait()
        return carry

    lax.fori_loop(0, tm, start, 0, unroll=8)
    lax.fori_loop(0, tm, wait, 0, unroll=8)
    info = info_ref[...]
    w0 = info[:, R_W0:R_W0 + 1]
    w1 = info[:, R_W1:R_W1 + 1]
    ff = w0 * buf[0] + w1 * buf[1]
    y = x_ref[...] + mod_ref[...][5:6] * ff
    o_ref[...] = _final_norm(y, fg_ref[...]) if final else y


def _combine(y_sorted, pos, x, mod, info, final_gain, final):
    b, s, d = x.shape
    tm = min(TM_COMBINE, s)
    nt = s // tm
    pos3 = pos.reshape(b * nt, 1, TOP_K * tm)
    return pl.pallas_call(
        functools.partial(_combine_kernel, tm=tm, final=final),
        grid=(b, nt),
        in_specs=[pl.BlockSpec((1, 1, TOP_K * tm), lambda bi, i: (bi * nt + i, 0, 0),
                               memory_space=pltpu.SMEM),
                  pl.BlockSpec(memory_space=pl.ANY),
                  pl.BlockSpec((None, tm, d), lambda bi, i: (bi, i, 0)),
                  pl.BlockSpec((None, N_MOD, d), lambda bi, i: (bi, 0, 0)),
                  pl.BlockSpec((tm, LANES), lambda bi, i: (bi * nt + i, 0)),
                  _resident((1, d))],
        out_specs=pl.BlockSpec((None, tm, d), lambda bi, i: (bi, i, 0)),
        out_shape=jax.ShapeDtypeStruct(x.shape, F32),
        scratch_shapes=[pltpu.VMEM((TOP_K, tm, d), F32), pltpu.SemaphoreType.DMA(())],
        compiler_params=_params("arbitrary", "arbitrary"),
        name="moe_combine",
    )(pos3, y_sorted, x, mod, info, final_gain.reshape(1, d))


def _moe_ffn(x, mod, router_w, router_b, w_gate, w_up, w_down, final_gain, final):
    b, s, d = x.shape
    n = b * s
    n_e = router_w.shape[-1]
    tm = TM_EXPERT
    hp, info, cnt = _router(x, mod, router_w, router_b)

    counts = cnt[0, :n_e].astype(jnp.int32)
    tiles = (counts + tm - 1) // tm
    tile_end = jnp.cumsum(tiles)
    row_start = (tile_end - tiles) * tm
    n_tiles = (TOP_K * n) // tm + n_e
    tile_expert = jnp.minimum(
        jnp.sum(jnp.arange(n_tiles, dtype=jnp.int32)[:, None] >= tile_end[None, :], axis=1),
        n_e - 1).astype(jnp.int32)
    n_used = tile_end[-1:].astype(jnp.int32)
    experts = info[:, R_E0:R_E1 + 1].astype(jnp.int32)
    ranks = info[:, R_RANK0:R_RANK1 + 1].astype(jnp.int32)
    pos = (row_start[experts] + ranks).reshape(-1)

    xs = _scatter_rows(hp, pos, n_tiles * tm)
    y_sorted = _grouped_experts(xs, tile_expert, n_used, w_gate, w_up, w_down)
    return _combine(y_sorted, pos, x, mod, info, final_gain, final)


def _rope_tables(seq_len):
    pos = jnp.arange(seq_len, dtype=jnp.int32)
    axis_pos = jnp.stack([pos // GRID_W, pos % GRID_W], axis=1).astype(F32)
    n_freq = HEAD_DIM // 4
    inv_freq = jnp.exp(-math.log(ROPE_THETA) * (2.0 * jnp.arange(n_freq, dtype=F32) / (HEAD_DIM // 2)))
    ang = axis_pos[:, :, None] * inv_freq[None, None, :]
    cos = jnp.cos(ang)
    sin = jnp.sin(ang)
    cos_t = jnp.concatenate([cos, cos], axis=-1).reshape(seq_len, HEAD_DIM)
    sin_t = jnp.concatenate([-sin, sin], axis=-1).reshape(seq_len, HEAD_DIM)
    return cos_t, sin_t


def kernel(x, c, w_mod, b_mod, w_in, q_norm_gain, k_norm_gain, conv_w, conv_b, lru_w_a, lru_b_a,
           lru_w_x, lru_b_x, lru_lambda, w_o_attn, w_o_lru, w_out, ffn_w_gate, ffn_w_up,
           ffn_w_down, router_w, router_b, moe_w_gate, moe_w_up, moe_w_down, final_gain):
    b, s, d = x.shape
    depth = w_in.shape[0]
    cos, sin = _rope_tables(s)
    mods = _modulation(c, w_mod, b_mod).reshape(depth, b, N_MOD, d)
    for l in range(depth):
        mod = mods[l]
        q, k, v, xr, gxg, sg = _inproj(x, mod, w_in[l].astype(BF16), q_norm_gain[l], k_norm_gain[l],
                                       cos, sin)
        ylru = _lru_branch(xr, gxg, conv_w[l], conv_b[l], lru_w_a[l], lru_b_a[l], lru_w_x[l],
                           lru_b_x[l], lru_lambda[l])
        attn = _attention(q, k, v)
        x = _merge(attn, ylru, sg, x, mod, w_o_attn[l].astype(BF16), w_o_lru[l].astype(BF16),
                   w_out[l].astype(BF16))
        final = l == depth - 1
        j = l // 2
        if l % 2 == 0:
            x = _dense_ffn(x, mod, ffn_w_gate[j].astype(BF16), ffn_w_up[j].astype(BF16),
                           ffn_w_down[j].astype(BF16), final_gain, final)
        else:
            x = _moe_ffn(x, mod, router_w[j], router_b[j], moe_w_gate[j].astype(BF16),
                         moe_w_up[j].astype(BF16), moe_w_down[j].astype(BF16), final_gain, final)
    return x
```

```python
import functools
import math

import jax
import jax.numpy as jnp
from jax import lax
from jax.experimental import pallas as pl
from jax.experimental.pallas import tpu as pltpu

HEAD_DIM = 128
N_Q_HEADS = 8
N_KV_HEADS = 2
Q_GROUP = N_Q_HEADS // N_KV_HEADS
GRID_W = 64
ROPE_THETA = 10000.0
ROPE_PAIR_SHIFT = HEAD_DIM // 4
LRU_BLOCKS = 8
LRU_C = 8.0
CONV_WIDTH = 4
N_MOD = 6
TOP_K = 2
NORM_EPS = 1e-6
LOG2_E = math.log2(math.e)
SOFTMAX_SAFE_LOG2_SHIFT = 55.0

LANES = 128
SUBLANES = 8
VMEM_LIMIT_BYTES = 56 * 1024 * 1024

TM_INPROJ = 256
TM_LRU = 256
TQ_ATTN = 128
TM_MERGE = 512
TM_FFN = 256
TM_ROUTER = 256
TM_EXPERT = 256
TM_SCATTER = 256
TM_COMBINE = 256
FF_CHUNK = 512
KV_CHUNK = 512

BF16 = jnp.bfloat16
F32 = jnp.float32


def _params(*semantics):
    return pltpu.CompilerParams(dimension_semantics=semantics, vmem_limit_bytes=VMEM_LIMIT_BYTES)


def _resident(shape):
    zeros = (0,) * len(shape)
    return pl.BlockSpec(shape, lambda *_: zeros)


def _modulated_norm(x, shift, scale):
    ms = jnp.mean(x * x, axis=-1, keepdims=True)
    return x * lax.rsqrt(ms + NORM_EPS) * (1.0 + scale) + shift


def _mod_kernel(c_ref, w_ref, b_ref, o_ref):
    c = c_ref[...]
    act = c * jax.nn.sigmoid(c)
    o_ref[...] = jnp.dot(act, w_ref[...], preferred_element_type=F32,
                         precision=lax.Precision.HIGHEST) + b_ref[...]


def _modulation(c, w_mod, b_mod):
    depth, d, n = w_mod.shape
    b = c.shape[0]
    tn = 1536
    return pl.pallas_call(
        _mod_kernel,
        grid=(depth, n // tn),
        in_specs=[pl.BlockSpec((b, d), lambda l, j: (0, 0)),
                  pl.BlockSpec((None, d, tn), lambda l, j: (l, 0, j)),
                  pl.BlockSpec((None, 1, tn), lambda l, j: (l, 0, j))],
        out_specs=pl.BlockSpec((None, b, tn), lambda l, j: (l, 0, j)),
        out_shape=jax.ShapeDtypeStruct((depth, b, n), F32),
        compiler_params=_params("parallel", "parallel"),
        name="modulation",
    )(c, w_mod, b_mod.reshape(depth, 1, n))


def _inproj_kernel(x_ref, mod_ref, w_ref, qg_ref, kg_ref, cos_ref, sin_ref,
                   q_ref, k_ref, v_ref, xr_ref, gxg_ref, sg_ref, *, d_model):
    attn_w = N_Q_HEADS * HEAD_DIM
    kv_w = N_KV_HEADS * HEAD_DIM
    mod = mod_ref[...]
    hb = _modulated_norm(x_ref[...], mod[0:1], mod[1:2]).astype(BF16)

    cos = cos_ref[...]
    sin = sin_ref[...]
    lane = lax.broadcasted_iota(jnp.int32, cos.shape, 1)
    pair_first = (lane & ROPE_PAIR_SHIFT) == 0

    def norm_rope(y, gain, post_scale):
        r = lax.rsqrt(jnp.mean(y * y, axis=-1, keepdims=True) + NORM_EPS)
        yn = y * r * gain
        partner = jnp.where(pair_first,
                            pltpu.roll(yn, HEAD_DIM - ROPE_PAIR_SHIFT, 1),
                            pltpu.roll(yn, ROPE_PAIR_SHIFT, 1))
        return (yn * cos + partner * sin) * post_scale

    def proj(lo, width):
        return jnp.dot(hb, w_ref[:, lo:lo + width], preferred_element_type=F32)

    qf = proj(0, attn_w)
    qg = qg_ref[...]
    for h in range(N_Q_HEADS):
        sl = slice(h * HEAD_DIM, (h + 1) * HEAD_DIM)
        q_ref[:, sl] = norm_rope(qf[:, sl], qg, HEAD_DIM ** -0.5 * LOG2_E).astype(BF16)
    kf = proj(attn_w, kv_w)
    kg = kg_ref[...]
    for h in range(N_KV_HEADS):
        sl = slice(h * HEAD_DIM, (h + 1) * HEAD_DIM)
        k_ref[:, sl] = norm_rope(kf[:, sl], kg, 1.0).astype(BF16)
    v_ref[...] = proj(attn_w + kv_w, kv_w).astype(BF16)
    lo = attn_w + 2 * kv_w
    xr_ref[...] = proj(lo, d_model)
    gxg_ref[...] = jax.nn.gelu(proj(lo + d_model, d_model), approximate=True).astype(BF16)
    sg_ref[...] = jax.nn.sigmoid(proj(lo + 2 * d_model, 2 * d_model)).astype(BF16)


def _inproj(x, mod, w_in, q_gain, k_gain, cos, sin):
    b, s, d = x.shape
    tm = min(TM_INPROJ, s)
    attn_w = N_Q_HEADS * HEAD_DIM
    kv_w = N_KV_HEADS * HEAD_DIM
    row = lambda w: pl.BlockSpec((None, tm, w), lambda bi, i: (bi, i, 0))
    tab = pl.BlockSpec((tm, HEAD_DIM), lambda bi, i: (i, 0))
    outs = [(attn_w, BF16), (kv_w, BF16), (kv_w, BF16), (d, F32), (d, BF16), (2 * d, BF16)]
    return pl.pallas_call(
        functools.partial(_inproj_kernel, d_model=d),
        grid=(b, s // tm),
        in_specs=[row(d),
                  pl.BlockSpec((None, N_MOD, d), lambda bi, i: (bi, 0, 0)),
                  _resident(w_in.shape), _resident((1, HEAD_DIM)), _resident((1, HEAD_DIM)),
                  tab, tab],
        out_specs=[row(w) for w, _ in outs],
        out_shape=[jax.ShapeDtypeStruct((b, s, w), dt) for w, dt in outs],
        compiler_params=_params("parallel", "parallel"),
        name="inproj",
    )(x, mod, w_in, q_gain.reshape(1, HEAD_DIM), k_gain.reshape(1, HEAD_DIM), cos, sin)


def _lru_kernel(*refs, ts, n_tiles, reverse, final):
    if final:
        (xr_ref, prev_ref, next_ref, cw_ref, cb_ref, wcat_ref, ba_ref, bx_ref, lam_ref,
         hb_ref, gxg_ref, o_ref, xs_scr, a_scr, u_scr, carry_scr) = refs
    else:
        (xr_ref, prev_ref, next_ref, cw_ref, cb_ref, wcat_ref, ba_ref, bx_ref, lam_ref,
         o_ref, xs_scr, a_scr, u_scr, carry_scr) = refs
    j = pl.program_id(1)
    tile = (n_tiles - 1 - j) if reverse else j
    bw = LANES

    @pl.when(j == 0)
    def _():
        carry_scr[...] = jnp.zeros_like(carry_scr)

    xs_scr[0:SUBLANES, :] = jnp.where(tile > 0, prev_ref[...], 0.0)
    xs_scr[SUBLANES:SUBLANES + ts, :] = xr_ref[...]
    xs_scr[SUBLANES + ts:, :] = jnp.where(tile < n_tiles - 1, next_ref[...], 0.0)
    cw = cw_ref[...]

    def tap(shift):
        return xs_scr[SUBLANES + shift:SUBLANES + shift + ts, :]

    xc = (cw[0:1] * tap(-2) + cw[1:2] * tap(-1) + cw[2:3] * tap(0) + cw[3:4] * tap(1)
          + cb_ref[...])

    lam = lam_ref[...]
    neg_c_softplus = -LRU_C * (jnp.maximum(-lam, 0.0) + jnp.log1p(jnp.exp(-jnp.abs(lam))))
    ba = ba_ref[...]
    bx = bx_ref[...]
    for n in range(LRU_BLOCKS):
        sl = slice(n * bw, (n + 1) * bw)
        xb = xc[:, sl]
        g = jnp.dot(xb.astype(BF16), wcat_ref[n], preferred_element_type=F32)
        r = jax.nn.sigmoid(g[:, :bw] + ba[:, sl])
        i = jax.nn.sigmoid(g[:, bw:] + bx[:, sl])
        log_a = neg_c_softplus[:, sl] * r
        a_scr[:, sl] = jnp.exp(log_a)
        th = jnp.tanh(log_a)
        u_scr[:, sl] = jnp.sqrt(-2.0 * th / (1.0 - th)) * (i * xb)

    row = lax.broadcasted_iota(jnp.int32, (SUBLANES, bw), 0)
    n_groups = ts // SUBLANES
    for n in range(LRU_BLOCKS):
        sl = slice(n * bw, (n + 1) * bw)
        state = carry_scr[0:1, sl]
        for v in (range(n_groups - 1, -1, -1) if reverse else range(n_groups)):
            rows = slice(v * SUBLANES, (v + 1) * SUBLANES)
            a = a_scr[rows, sl]
            u = u_scr[rows, sl]
            for d in (1, 2, 4):
                keep = (row < SUBLANES - d) if reverse else (row >= d)
                shift = (SUBLANES - d) if reverse else d
                a_prev = jnp.where(keep, pltpu.roll(a, shift, 0), 1.0)
                u_prev = jnp.where(keep, pltpu.roll(u, shift, 0), 0.0)
                u = a * u_prev + u
                a = a * a_prev
            h = u + a * state
            u_scr[rows, sl] = h
            state = h[0:1] if reverse else h[SUBLANES - 1:SUBLANES]
        carry_scr[0:1, sl] = state
    if final:
        o_ref[...] = ((u_scr[...] + hb_ref[...]) * gxg_ref[...].astype(F32)).astype(BF16)
    else:
        o_ref[...] = u_scr[...]


def _lru_pass(xr, conv_w, conv_b, wcat, b_a, b_x, lam, reverse, h_other=None, gxg=None):
    b, s, w = xr.shape
    ts = min(TM_LRU, s)
    n_tiles = s // ts
    blocks8 = ts // SUBLANES
    final = h_other is not None
    tile = (lambda j: n_tiles - 1 - j) if reverse else (lambda j: j)
    main = pl.BlockSpec((None, ts, w), lambda bi, j: (bi, tile(j), 0))
    prev = pl.BlockSpec((None, SUBLANES, w),
                        lambda bi, j: (bi, jnp.maximum(tile(j) * blocks8 - 1, 0), 0))
    nxt = pl.BlockSpec((None, SUBLANES, w),
                       lambda bi, j: (bi, jnp.minimum((tile(j) + 1) * blocks8, s // SUBLANES - 1), 0))
    vec = _resident((1, w))
    in_specs = [main, prev, nxt, _resident((CONV_WIDTH, w)), vec, _resident(wcat.shape), vec, vec, vec]
    args = [xr, xr, xr, conv_w, conv_b.reshape(1, w), wcat, b_a.reshape(1, w), b_x.reshape(1, w),
            lam.reshape(1, w)]
    if final:
        in_specs += [main, main]
        args += [h_other, gxg]
    return pl.pallas_call(
        functools.partial(_lru_kernel, ts=ts, n_tiles=n_tiles, reverse=reverse, final=final),
        grid=(b, n_tiles),
        in_specs=in_specs,
        out_specs=main,
        out_shape=jax.ShapeDtypeStruct((b, s, w), BF16 if final else F32),
        scratch_shapes=[pltpu.VMEM((ts + 2 * SUBLANES, w), F32), pltpu.VMEM((ts, w), F32),
                        pltpu.VMEM((ts, w), F32), pltpu.VMEM((SUBLANES, w), F32)],
        compiler_params=_params("parallel", "arbitrary"),
        name="lru_fwd" if final else "lru_bwd",
    )(*args)


def _lru_branch(xr, gxg, conv_w, conv_b, w_a, b_a, w_x, b_x, lam):
    wcat = jnp.concatenate([w_a, w_x], axis=-1).astype(BF16)
    h_bwd = _lru_pass(xr, conv_w, conv_b, wcat[1], b_a[1], b_x[1], lam[1], reverse=True)
    return _lru_pass(xr, conv_w, conv_b, wcat[0], b_a[0], b_x[0], lam[0], reverse=False,
                     h_other=h_bwd, gxg=gxg)


def _attn_kernel(q_ref, k_ref, v_ref, o_ref, vext_scr, kmax_scr, *, tq, s_len, kv_chunk):
    nt = (((1,), (1,)), ((), ()))

    @pl.when(pl.program_id(2) == 0)
    def _():
        vext_scr[:, :HEAD_DIM] = v_ref[...]
        vext_scr[:, HEAD_DIM:] = jnp.ones((s_len, HEAD_DIM), BF16)
        kf = k_ref[...].astype(F32)
        k2 = jnp.max(jnp.sum(kf * kf, axis=-1, keepdims=True), axis=0, keepdims=True)
        kmax_scr[...] = jnp.broadcast_to(jnp.sqrt(k2), kmax_scr.shape)

    q = q_ref[...]
    qs = jnp.concatenate([q[:, h * HEAD_DIM:(h + 1) * HEAD_DIM] for h in range(Q_GROUP)], axis=0)
    qf = qs.astype(F32)
    bound = jnp.sqrt(jnp.sum(qf * qf, axis=-1, keepdims=True)) * kmax_scr[0:1, 0:1]
    safe = jnp.max(bound) <= SOFTMAX_SAFE_LOG2_SHIFT

    def finish(acc):
        o = acc[:, :HEAD_DIM] / acc[:, HEAD_DIM:]
        for h in range(Q_GROUP):
            o_ref[:, h * HEAD_DIM:(h + 1) * HEAD_DIM] = o[h * tq:(h + 1) * tq].astype(BF16)

    @pl.when(safe)
    def _():
        acc = None
        for lo in range(0, s_len, kv_chunk):
            s = lax.dot_general(qs, k_ref[lo:lo + kv_chunk, :], nt, preferred_element_type=F32)
            p = jnp.exp2(s - bound).astype(BF16)
            pv = jnp.dot(p, vext_scr[lo:lo + kv_chunk, :], preferred_element_type=F32)
            acc = pv if acc is None else acc + pv
        finish(acc)

    @pl.when(jnp.logical_not(safe))
    def _():
        s = lax.dot_general(qs, k_ref[...], nt, preferred_element_type=F32)
        p = jnp.exp2(s - jnp.max(s, axis=-1, keepdims=True)).astype(BF16)
        finish(jnp.dot(p, vext_scr[...], preferred_element_type=F32))


def _attention(q, k, v):
    b, s, _ = q.shape
    tq = min(TQ_ATTN, s)
    gw = Q_GROUP * HEAD_DIM
    qspec = pl.BlockSpec((None, tq, gw), lambda bi, g, i: (bi, i, g))
    kvspec = pl.BlockSpec((None, s, HEAD_DIM), lambda bi, g, i: (bi, 0, g))
    return pl.pallas_call(
        functools.partial(_attn_kernel, tq=tq, s_len=s, kv_chunk=min(KV_CHUNK, s)),
        grid=(b, N_KV_HEADS, s // tq),
        in_specs=[qspec, kvspec, kvspec],
        out_specs=qspec,
        out_shape=jax.ShapeDtypeStruct(q.shape, BF16),
        scratch_shapes=[pltpu.VMEM((s, 2 * HEAD_DIM), BF16), pltpu.VMEM((SUBLANES, LANES), F32)],
        compiler_params=_params("parallel", "parallel", "arbitrary"),
        name="attention",
    )(q, k, v)


def _merge_kernel(attn_ref, ylru_ref, sg_ref, x_ref, mod_ref, woa_ref, wol_ref, wout_ref, o_ref,
                  *, d_model):
    ya = jnp.dot(attn_ref[...], woa_ref[...], preferred_element_type=F32)
    yl = jnp.dot(ylru_ref[...], wol_ref[...], preferred_element_type=F32)
    sg = sg_ref[...].astype(F32)
    merged = sg[:, :d_model] * ya + sg[:, d_model:] * yl
    out = jnp.dot(merged.astype(BF16), wout_ref[...], preferred_element_type=F32)
    o_ref[...] = x_ref[...] + mod_ref[...][2:3] * out


def _merge(attn, ylru, sg, x, mod, w_o_attn, w_o_lru, w_out):
    b, s, d = x.shape
    tm = min(TM_MERGE, s)
    row = lambda w: pl.BlockSpec((None, tm, w), lambda bi, i: (bi, i, 0))
    return pl.pallas_call(
        functools.partial(_merge_kernel, d_model=d),
        grid=(b, s // tm),
        in_specs=[row(attn.shape[-1]), row(d), row(2 * d), row(d),
                  pl.BlockSpec((None, N_MOD, d), lambda bi, i: (bi, 0, 0)),
                  _resident(w_o_attn.shape), _resident(w_o_lru.shape), _resident(w_out.shape)],
        out_specs=row(d),
        out_shape=jax.ShapeDtypeStruct(x.shape, F32),
        compiler_params=_params("parallel", "parallel"),
        name="merge",
    )(attn, ylru, sg, x, mod, w_o_attn, w_o_lru, w_out)


def _swiglu_chunks(hb, wg_ref, wu_ref, wd_ref):
    d_ff = wg_ref.shape[-1]
    acc = None
    for lo in range(0, d_ff, FF_CHUNK):
        hi = min(lo + FF_CHUNK, d_ff)
        g = jnp.dot(hb, wg_ref[:, lo:hi], preferred_element_type=F32)
        u = jnp.dot(hb, wu_ref[:, lo:hi], preferred_element_type=F32)
        act = (g * jax.nn.sigmoid(g) * u).astype(BF16)
        y = jnp.dot(act, wd_ref[lo:hi, :], preferred_element_type=F32)
        acc = y if acc is None else acc + y
    return acc


def _final_norm(x, gain):
    return x * lax.rsqrt(jnp.mean(x * x, axis=-1, keepdims=True) + NORM_EPS) * gain


def _ffn_kernel(x_ref, mod_ref, wg_ref, wu_ref, wd_ref, fg_ref, o_ref, *, final):
    x = x_ref[...]
    mod = mod_ref[...]
    hb = _modulated_norm(x, mod[3:4], mod[4:5]).astype(BF16)
    y = x + mod[5:6] * _swiglu_chunks(hb, wg_ref, wu_ref, wd_ref)
    o_ref[...] = _final_norm(y, fg_ref[...]) if final else y


def _dense_ffn(x, mod, w_gate, w_up, w_down, final_gain, final):
    b, s, d = x.shape
    tm = min(TM_FFN, s)
    row = pl.BlockSpec((None, tm, d), lambda bi, i: (bi, i, 0))
    return pl.pallas_call(
        functools.partial(_ffn_kernel, final=final),
        grid=(b, s // tm),
        in_specs=[row, pl.BlockSpec((None, N_MOD, d), lambda bi, i: (bi, 0, 0)),
                  _resident(w_gate.shape), _resident(w_up.shape), _resident(w_down.shape),
                  _resident((1, d))],
        out_specs=row,
        out_shape=jax.ShapeDtypeStruct(x.shape, F32),
        compiler_params=_params("parallel", "parallel"),
        name="dense_ffn",
    )(x, mod, w_gate, w_up, w_down, final_gain.reshape(1, d))


R_E0, R_E1, R_RANK0, R_RANK1, R_W0, R_W1 = range(6)


def _router_kernel(x_ref, mod_ref, rw_ref, rb_ref, hp_ref, info_ref, cnt_ref, carry_scr, *, tm):
    @pl.when((pl.program_id(0) == 0) & (pl.program_id(1) == 0))
    def _():
        carry_scr[...] = jnp.zeros_like(carry_scr)

    mod = mod_ref[...]
    h = _modulated_norm(x_ref[...], mod[3:4], mod[4:5])
    hp_ref[...] = h

    logits = jnp.dot(h, rw_ref[...], preferred_element_type=F32,
                     precision=lax.Precision.HIGHEST) + rb_ref[...]
    lane = lax.broadcasted_iota(jnp.int32, logits.shape, 1)
    m1 = jnp.max(logits, axis=-1, keepdims=True)
    e0 = jnp.min(jnp.where(logits == m1, lane, LANES), axis=-1, keepdims=True)
    rest = jnp.where(lane == e0, -jnp.inf, logits)
    m2 = jnp.max(rest, axis=-1, keepdims=True)
    e1 = jnp.min(jnp.where(rest == m2, lane, LANES), axis=-1, keepdims=True)
    t = jnp.exp(m2 - m1)
    w0 = 1.0 / (1.0 + t)
    w1 = t * w0

    hit0 = lane == e0
    hit1 = lane == e1
    onehot = jnp.where(hit0 | hit1, 1.0, 0.0)
    ri = lax.broadcasted_iota(jnp.int32, (tm, tm), 0)
    ci = lax.broadcasted_iota(jnp.int32, (tm, tm), 1)
    before = jnp.where(ci < ri, 1.0, 0.0).astype(BF16)
    seen = jnp.dot(before, onehot.astype(BF16), preferred_element_type=F32) + carry_scr[0:1, :]
    rank0 = jnp.sum(jnp.where(hit0, seen, 0.0), axis=-1, keepdims=True)
    rank1 = jnp.sum(jnp.where(hit1, seen, 0.0), axis=-1, keepdims=True)
    carry = carry_scr[0:1, :] + jnp.sum(onehot, axis=0, keepdims=True)
    carry_scr[0:1, :] = carry
    cnt_ref[...] = jnp.broadcast_to(carry, cnt_ref.shape)

    info = jnp.zeros(logits.shape, F32)
    for col, val in ((R_E0, e0.astype(F32)), (R_E1, e1.astype(F32)), (R_RANK0, rank0),
                     (R_RANK1, rank1), (R_W0, w0), (R_W1, w1)):
        info = jnp.where(lane == col, val, info)
    info_ref[...] = info


def _router(x, mod, router_w, router_b):
    b, s, d = x.shape
    tm = min(TM_ROUTER, s)
    n_e = router_w.shape[-1]
    rw = jnp.zeros((d, LANES), F32).at[:, :n_e].set(router_w)
    rb = jnp.full((1, LANES), -jnp.inf, F32).at[0, :n_e].set(router_b)
    nt = s // tm
    return pl.pallas_call(
        functools.partial(_router_kernel, tm=tm),
        grid=(b, nt),
        in_specs=[pl.BlockSpec((None, tm, d), lambda bi, i: (bi, i, 0)),
                  pl.BlockSpec((None, N_MOD, d), lambda bi, i: (bi, 0, 0)),
                  _resident(rw.shape), _resident(rb.shape)],
        out_specs=[pl.BlockSpec((tm, d), lambda bi, i: (bi * nt + i, 0)),
                   pl.BlockSpec((tm, LANES), lambda bi, i: (bi * nt + i, 0)),
                   _resident((SUBLANES, LANES))],
        out_shape=[jax.ShapeDtypeStruct((b * s, d), F32),
                   jax.ShapeDtypeStruct((b * s, LANES), F32),
                   jax.ShapeDtypeStruct((SUBLANES, LANES), F32)],
        scratch_shapes=[pltpu.VMEM((SUBLANES, LANES), F32)],
        compiler_params=_params("arbitrary", "arbitrary"),
        name="router",
    )(x, mod, rw, rb)


def _scatter_kernel(pos_ref, hp_ref, init_ref, xs_ref, sem, *, tm):
    del init_ref

    def copy(r, k):
        return pltpu.make_async_copy(hp_ref.at[pl.ds(r, 1)],
                                     xs_ref.at[pl.ds(pos_ref[0, 0, TOP_K * r + k], 1)], sem)

    def start(r, carry):
        for k in range(TOP_K):
            copy(r, k).start()
        return carry

    def wait(r, carry):
        for k in range(TOP_K):
            copy(r, k).wait()
        return carry

    lax.fori_loop(0, tm, start, 0, unroll=8)
    lax.fori_loop(0, tm, wait, 0, unroll=8)


def _scatter_rows(hp, pos, n_rows):
    n, w = hp.shape
    tm = min(TM_SCATTER, n)
    pos3 = pos.reshape(n // tm, 1, TOP_K * tm)
    return pl.pallas_call(
        functools.partial(_scatter_kernel, tm=tm),
        grid=(n // tm,),
        in_specs=[pl.BlockSpec((1, 1, TOP_K * tm), lambda i: (i, 0, 0), memory_space=pltpu.SMEM),
                  pl.BlockSpec((tm, w), lambda i: (i, 0)),
                  pl.BlockSpec(memory_space=pl.ANY)],
        out_specs=pl.BlockSpec(memory_space=pl.ANY),
        out_shape=jax.ShapeDtypeStruct((n_rows, w), hp.dtype),
        scratch_shapes=[pltpu.SemaphoreType.DMA(())],
        input_output_aliases={2: 0},
        compiler_params=_params("arbitrary"),
        name="moe_scatter",
    )(pos3, hp, jnp.zeros((n_rows, w), hp.dtype))


def _expert_kernel(te_ref, nu_ref, xs_ref, wg_ref, wu_ref, wd_ref, y_ref):
    del te_ref
    i = pl.program_id(0)

    @pl.when(i < nu_ref[0])
    def _():
        y_ref[...] = _swiglu_chunks(xs_ref[...].astype(BF16), wg_ref, wu_ref, wd_ref)

    @pl.when(i >= nu_ref[0])
    def _():
        y_ref[...] = jnp.zeros_like(y_ref)


def _grouped_experts(xs, tile_expert, n_used, w_gate, w_up, w_down):
    n_rows, d = xs.shape
    tm = TM_EXPERT
    ff = w_gate.shape[-1]
    grid_spec = pltpu.PrefetchScalarGridSpec(
        num_scalar_prefetch=2,
        grid=(n_rows // tm,),
        in_specs=[pl.BlockSpec((tm, d), lambda i, te, nu: (i, 0)),
                  pl.BlockSpec((None, d, ff), lambda i, te, nu: (te[i], 0, 0)),
                  pl.BlockSpec((None, d, ff), lambda i, te, nu: (te[i], 0, 0)),
                  pl.BlockSpec((None, ff, d), lambda i, te, nu: (te[i], 0, 0))],
        out_specs=pl.BlockSpec((tm, d), lambda i, te, nu: (i, 0)),
    )
    return pl.pallas_call(
        _expert_kernel,
        grid_spec=grid_spec,
        out_shape=jax.ShapeDtypeStruct((n_rows, d), F32),
        compiler_params=pltpu.CompilerParams(dimension_semantics=("arbitrary",),
                                             vmem_limit_bytes=60 * 1024 * 1024),
        name="moe_experts",
    )(tile_expert, n_used, xs, w_gate, w_up, w_down)


def _combine_kernel(pos_ref, y_ref, x_ref, mod_ref, info_ref, fg_ref, o_ref, buf, sem, *, tm, final):
    def copy(r, k):
        return pltpu.make_async_copy(y_ref.at[pl.ds(pos_ref[0, 0, TOP_K * r + k], 1)],
                                     buf.at[k, pl.ds(r, 1)], sem)

    def start(r, carry):
        for k in range(TOP_K):
            copy(r, k).start()
        return carry

    def wait(r, carry):
        for k in range(TOP_K):
            copy(r, k).wait()
        return carry

    lax.fori_loop(0, tm, start, 0, unroll=8)
    lax.fori_loop(0, tm, wait, 0, unroll=8)
    info = info_ref[...]
    w0 = info[:, R_W0:R_W0 + 1]
    w1 = info[:, R_W1:R_W1 + 1]
    ff = w0 * buf[0] + w1 * buf[1]
    y = x_ref[...] + mod_ref[...][5:6] * ff
    o_ref[...] = _final_norm(y, fg_ref[...]) if final else y


def _combine(y_sorted, pos, x, mod, info, final_gain, final):
    b, s, d = x.shape
    tm = min(TM_COMBINE, s)
    nt = s // tm
    pos3 = pos.reshape(b * nt, 1, TOP_K * tm)
    return pl.pallas_call(
        functools.partial(_combine_kernel, tm=tm, final=final),
        grid=(b, nt),
        in_specs=[pl.BlockSpec((1, 1, TOP_K * tm), lambda bi, i: (bi * nt + i, 0, 0),
                               memory_space=pltpu.SMEM),
                  pl.BlockSpec(memory_space=pl.ANY),
                  pl.BlockSpec((None, tm, d), lambda bi, i: (bi, i, 0)),
                  pl.BlockSpec((None, N_MOD, d), lambda bi, i: (bi, 0, 0)),
                  pl.BlockSpec((tm, LANES), lambda bi, i: (bi * nt + i, 0)),
                  _resident((1, d))],
        out_specs=pl.BlockSpec((None, tm, d), lambda bi, i: (bi, i, 0)),
        out_shape=jax.ShapeDtypeStruct(x.shape, F32),
        scratch_shapes=[pltpu.VMEM((TOP_K, tm, d), F32), pltpu.SemaphoreType.DMA(())],
        compiler_params=_params("arbitrary", "arbitrary"),
        name="moe_combine",
    )(pos3, y_sorted, x, mod, info, final_gain.reshape(1, d))


def _moe_ffn(x, mod, router_w, router_b, w_gate, w_up, w_down, final_gain, final):
    b, s, d = x.shape
    n = b * s
    n_e = router_w.shape[-1]
    tm = TM_EXPERT
    hp, info, cnt = _router(x, mod, router_w, router_b)

    counts = cnt[0, :n_e].astype(jnp.int32)
    tiles = (counts + tm - 1) // tm
    tile_end = jnp.cumsum(tiles)
    row_start = (tile_end - tiles) * tm
    n_tiles = (TOP_K * n) // tm + n_e
    tile_expert = jnp.minimum(
        jnp.sum(jnp.arange(n_tiles, dtype=jnp.int32)[:, None] >= tile_end[None, :], axis=1),
        n_e - 1).astype(jnp.int32)
    n_used = tile_end[-1:].astype(jnp.int32)
    experts = info[:, R_E0:R_E1 + 1].astype(jnp.int32)
    ranks = info[:, R_RANK0:R_RANK1 + 1].astype(jnp.int32)
    pos = (row_start[experts] + ranks).reshape(-1)

    xs = _scatter_rows(hp, pos, n_tiles * tm)
    y_sorted = _grouped_experts(xs, tile_expert, n_used, w_gate, w_up, w_down)
    return _combine(y_sorted, pos, x, mod, info, final_gain, final)


def _rope_tables(seq_len):
    pos = jnp.arange(seq_len, dtype=jnp.int32)
    axis_pos = jnp.stack([pos // GRID_W, pos % GRID_W], axis=1).astype(F32)
    n_freq = HEAD_DIM // 4
    inv_freq = jnp.exp(-math.log(ROPE_THETA) * (2.0 * jnp.arange(n_freq, dtype=F32) / (HEAD_DIM // 2)))
    ang = axis_pos[:, :, None] * inv_freq[None, None, :]
    cos = jnp.cos(ang)
    sin = jnp.sin(ang)
    cos_t = jnp.concatenate([cos, cos], axis=-1).reshape(seq_len, HEAD_DIM)
    sin_t = jnp.concatenate([-sin, sin], axis=-1).reshape(seq_len, HEAD_DIM)
    return cos_t, sin_t


def kernel(x, c, w_mod, b_mod, w_in, q_norm_gain, k_norm_gain, conv_w, conv_b, lru_w_a, lru_b_a,
           lru_w_x, lru_b_x, lru_lambda, w_o_attn, w_o_lru, w_out, ffn_w_gate, ffn_w_up,
           ffn_w_down, router_w, router_b, moe_w_gate, moe_w_up, moe_w_down, final_gain):
    b, s, d = x.shape
    depth = w_in.shape[0]
    cos, sin = _rope_tables(s)
    mods = _modulation(c, w_mod, b_mod).reshape(depth, b, N_MOD, d)
    for l in range(depth):
        mod = mods[l]
        q, k, v, xr, gxg, sg = _inproj(x, mod, w_in[l].astype(BF16), q_norm_gain[l], k_norm_gain[l],
                                       cos, sin)
        ylru = _lru_branch(xr, gxg, conv_w[l], conv_b[l], lru_w_a[l], lru_b_a[l], lru_w_x[l],
                           lru_b_x[l], lru_lambda[l])
        attn = _attention(q, k, v)
        x = _merge(attn, ylru, sg, x, mod, w_o_attn[l].astype(BF16), w_o_lru[l].astype(BF16),
                   w_out[l].astype(BF16))
        final = l == depth - 1
        j = l // 2
        if l % 2 == 0:
            x = _dense_ffn(x, mod, ffn_w_gate[j].astype(BF16), ffn_w_up[j].astype(BF16),
                           ffn_w_down[j].astype(BF16), final_gain, final)
        else:
            x = _moe_ffn(x, mod, router_w[j], router_b[j], moe_w_gate[j].astype(BF16),
                         moe_w_up[j].astype(BF16), moe_w_down[j].astype(BF16), final_gain, final)
    return x
```

```python
import functools
import math

import jax
import jax.numpy as jnp
from jax import lax
from jax.experimental import pallas as pl
from jax.experimental.pallas import tpu as pltpu

HEAD_DIM = 128
N_Q_HEADS = 8
N_KV_HEADS = 2
Q_GROUP = N_Q_HEADS // N_KV_HEADS
GRID_W = 64
ROPE_THETA = 10000.0
ROPE_PAIR_SHIFT = HEAD_DIM // 4
LRU_BLOCKS = 8
LRU_C = 8.0
CONV_WIDTH = 4
N_MOD = 6
TOP_K = 2
NORM_EPS = 1e-6
LOG2_E = math.log2(math.e)
SOFTMAX_SAFE_LOG2_SHIFT = 55.0

LANES = 128
SUBLANES = 8
VMEM_LIMIT_BYTES = 56 * 1024 * 1024

TM_INPROJ = 256
TQ_ATTN = 128
TM_MERGE = 512
TM_FFN = 256
TM_ROUTER = 256
TM_EXPERT = 256
TM_SCATTER = 256
TM_COMBINE = 256
FF_CHUNK = 512
KV_CHUNK = 512

BF16 = jnp.bfloat16
F32 = jnp.float32


def _params(*semantics):
    return pltpu.CompilerParams(dimension_semantics=semantics, vmem_limit_bytes=VMEM_LIMIT_BYTES)


def _resident(shape):
    zeros = (0,) * len(shape)
    return pl.BlockSpec(shape, lambda *_: zeros)


def _modulated_norm(x, shift, scale):
    ms = jnp.mean(x * x, axis=-1, keepdims=True)
    return x * lax.rsqrt(ms + NORM_EPS) * (1.0 + scale) + shift


def _mod_kernel(c_ref, w_ref, b_ref, o_ref):
    c = c_ref[...]
    act = c * jax.nn.sigmoid(c)
    o_ref[...] = jnp.dot(act, w_ref[...], preferred_element_type=F32,
                         precision=lax.Precision.HIGHEST) + b_ref[...]


def _modulation(c, w_mod, b_mod):
    depth, d, n = w_mod.shape
    b = c.shape[0]
    tn = 1536
    return pl.pallas_call(
        _mod_kernel,
        grid=(depth, n // tn),
        in_specs=[pl.BlockSpec((b, d), lambda l, j: (0, 0)),
                  pl.BlockSpec((None, d, tn), lambda l, j: (l, 0, j)),
                  pl.BlockSpec((None, 1, tn), lambda l, j: (l, 0, j))],
        out_specs=pl.BlockSpec((None, b, tn), lambda l, j: (l, 0, j)),
        out_shape=jax.ShapeDtypeStruct((depth, b, n), F32),
        compiler_params=_params("parallel", "parallel"),
        name="modulation",
    )(c, w_mod, b_mod.reshape(depth, 1, n))


def _inproj_kernel(x_ref, mod_ref, w_ref, qg_ref, kg_ref, cos_ref, sin_ref,
                   q_ref, k_ref, v_ref, xr_ref, gxg_ref, sg_ref, *, d_model):
    attn_w = N_Q_HEADS * HEAD_DIM
    kv_w = N_KV_HEADS * HEAD_DIM
    mod = mod_ref[...]
    hb = _modulated_norm(x_ref[...], mod[0:1], mod[1:2]).astype(BF16)

    cos = cos_ref[...]
    sin = sin_ref[...]
    lane = lax.broadcasted_iota(jnp.int32, cos.shape, 1)
    pair_first = (lane & ROPE_PAIR_SHIFT) == 0

    def norm_rope(y, gain, post_scale):
        r = lax.rsqrt(jnp.mean(y * y, axis=-1, keepdims=True) + NORM_EPS)
        yn = y * r * gain
        partner = jnp.where(pair_first,
                            pltpu.roll(yn, HEAD_DIM - ROPE_PAIR_SHIFT, 1),
                            pltpu.roll(yn, ROPE_PAIR_SHIFT, 1))
        return (yn * cos + partner * sin) * post_scale

    def proj(lo, width):
        return jnp.dot(hb, w_ref[:, lo:lo + width], preferred_element_type=F32)

    qf = proj(0, attn_w)
    qg = qg_ref[...]
    for h in range(N_Q_HEADS):
        sl = slice(h * HEAD_DIM, (h + 1) * HEAD_DIM)
        q_ref[:, sl] = norm_rope(qf[:, sl], qg, HEAD_DIM ** -0.5 * LOG2_E).astype(BF16)
    kf = proj(attn_w, kv_w)
    kg = kg_ref[...]
    for h in range(N_KV_HEADS):
        sl = slice(h * HEAD_DIM, (h + 1) * HEAD_DIM)
        k_ref[:, sl] = norm_rope(kf[:, sl], kg, 1.0).astype(BF16)
    v_ref[...] = proj(attn_w + kv_w, kv_w).astype(BF16)
    lo = attn_w + 2 * kv_w
    xr_ref[...] = proj(lo, d_model)
    gxg_ref[...] = jax.nn.gelu(proj(lo + d_model, d_model), approximate=True).astype(BF16)
    sg_ref[...] = jax.nn.sigmoid(proj(lo + 2 * d_model, 2 * d_model)).astype(BF16)


def _inproj(x, mod, w_in, q_gain, k_gain, cos, sin):
    b, s, d = x.shape
    tm = min(TM_INPROJ, s)
    attn_w = N_Q_HEADS * HEAD_DIM
    kv_w = N_KV_HEADS * HEAD_DIM
    row = lambda w: pl.BlockSpec((None, tm, w), lambda bi, i: (bi, i, 0))
    tab = pl.BlockSpec((tm, HEAD_DIM), lambda bi, i: (i, 0))
    outs = [(attn_w, BF16), (kv_w, BF16), (kv_w, BF16), (d, F32), (d, BF16), (2 * d, BF16)]
    return pl.pallas_call(
        functools.partial(_inproj_kernel, d_model=d),
        grid=(b, s // tm),
        in_specs=[row(d),
                  pl.BlockSpec((None, N_MOD, d), lambda bi, i: (bi, 0, 0)),
                  _resident(w_in.shape), _resident((1, HEAD_DIM)), _resident((1, HEAD_DIM)),
                  tab, tab],
        out_specs=[row(w) for w, _ in outs],
        out_shape=[jax.ShapeDtypeStruct((b, s, w), dt) for w, dt in outs],
        compiler_params=_params("parallel", "parallel"),
        name="inproj",
    )(x, mod, w_in, q_gain.reshape(1, HEAD_DIM), k_gain.reshape(1, HEAD_DIM), cos, sin)


def _lru_stage(xr_ref, prev_ref, next_ref, xs_scr, *, ts, tile, n_tiles):
    xs_scr[0:SUBLANES, :] = jnp.where(tile > 0, prev_ref[...], 0.0)
    xs_scr[SUBLANES:SUBLANES + ts, :] = xr_ref[...]
    xs_scr[SUBLANES + ts:, :] = jnp.where(tile < n_tiles - 1, next_ref[...], 0.0)


def _lru_block(n, cw_ref, cb_ref, wcat_ref, ba_ref, bx_ref, lam_ref, h_ref, xs_scr, carry_scr,
               *, ts, reverse):
    bw = LANES
    sl = slice(n * bw, (n + 1) * bw)

    def tap(shift):
        return xs_scr[SUBLANES + shift:SUBLANES + shift + ts, sl]

    cw = cw_ref[:, sl]
    xb = (cw[0:1] * tap(-2) + cw[1:2] * tap(-1) + cw[2:3] * tap(0) + cw[3:4] * tap(1)
          + cb_ref[:, sl])
    lam = lam_ref[:, sl]
    neg_c_softplus = -LRU_C * (jnp.maximum(-lam, 0.0) + jnp.log1p(jnp.exp(-jnp.abs(lam))))
    g = jnp.dot(xb.astype(BF16), wcat_ref[n], preferred_element_type=F32)
    r = jax.nn.sigmoid(g[:, :bw] + ba_ref[:, sl])
    i = jax.nn.sigmoid(g[:, bw:] + bx_ref[:, sl])
    log_a = neg_c_softplus * r
    a_all = jnp.exp(log_a)
    th = jnp.tanh(log_a)
    u_all = jnp.exp(0.5 * jnp.log(-2.0 * th / (1.0 - th))) * (i * xb)

    row = lax.broadcasted_iota(jnp.int32, (SUBLANES, bw), 0)
    n_groups = ts // SUBLANES
    state = carry_scr[0:1, sl]
    states = [None] * n_groups
    for v in (range(n_groups - 1, -1, -1) if reverse else range(n_groups)):
        rows = slice(v * SUBLANES, (v + 1) * SUBLANES)
        a = a_all[rows]
        u = u_all[rows]
        for d in (1, 2, 4):
            keep = (row < SUBLANES - d) if reverse else (row >= d)
            shift = (SUBLANES - d) if reverse else d
            a_prev = jnp.where(keep, pltpu.roll(a, shift, 0), 1.0)
            u_prev = jnp.where(keep, pltpu.roll(u, shift, 0), 0.0)
            u = a * u_prev + u
            a = a * a_prev
        h = u + a * state
        states[v] = h
        state = h[0:1] if reverse else h[SUBLANES - 1:SUBLANES]
    carry_scr[0:1, sl] = state
    h_ref[:, sl] = jnp.concatenate(states, axis=0).astype(BF16)


def _mixer_kernel(q_ref, k_ref, v_ref, xr_ref, prev_ref, next_ref, cw_ref, cb_ref, wcat_ref, ba_ref,
                  bx_ref, lam_ref, o_ref, h_ref, vext_scr, kmax_scr, xs_scr, carry_scr,
                  *, tq, s_len, kv_chunk, n_tiles, reverse):
    nt = (((1,), (1,)), ((), ()))
    i = pl.program_id(1)

    @pl.when(i == 0)
    def _():
        carry_scr[...] = jnp.zeros_like(carry_scr)
        vext_scr[:, :HEAD_DIM] = v_ref[...]
        vext_scr[:, HEAD_DIM:] = jnp.ones((s_len, HEAD_DIM), BF16)
        kf = k_ref[...].astype(F32)
        k2 = jnp.max(jnp.sum(kf * kf, axis=-1, keepdims=True), axis=0, keepdims=True)
        kmax_scr[...] = jnp.broadcast_to(jnp.sqrt(k2), kmax_scr.shape)

    lru_stage = functools.partial(_lru_stage, xr_ref, prev_ref, next_ref, xs_scr, ts=tq,
                                  tile=(n_tiles - 1 - i) if reverse else i, n_tiles=n_tiles)
    lru_block = functools.partial(_lru_block, cw_ref=cw_ref, cb_ref=cb_ref, wcat_ref=wcat_ref,
                                  ba_ref=ba_ref, bx_ref=bx_ref, lam_ref=lam_ref, h_ref=h_ref,
                                  xs_scr=xs_scr, carry_scr=carry_scr, ts=tq, reverse=reverse)

    q = q_ref[...]
    qs = jnp.concatenate([q[:, h * HEAD_DIM:(h + 1) * HEAD_DIM] for h in range(Q_GROUP)], axis=0)
    qf = qs.astype(F32)
    bound = jnp.sqrt(jnp.sum(qf * qf, axis=-1, keepdims=True)) * kmax_scr[0:1, 0:1]
    safe = jnp.max(bound) <= SOFTMAX_SAFE_LOG2_SHIFT

    def finish(acc):
        o = acc[:, :HEAD_DIM] / acc[:, HEAD_DIM:]
        for h in range(Q_GROUP):
            o_ref[:, h * HEAD_DIM:(h + 1) * HEAD_DIM] = o[h * tq:(h + 1) * tq].astype(BF16)

    @pl.when(safe)
    def _():
        lru_stage()
        chunks = list(range(0, s_len, kv_chunk))
        per_chunk = -(-LRU_BLOCKS // len(chunks))
        acc = None
        for c, lo in enumerate(chunks):
            s = lax.dot_general(qs, k_ref[lo:lo + kv_chunk, :], nt, preferred_element_type=F32)
            p = jnp.exp2(s - bound).astype(BF16)
            pv = jnp.dot(p, vext_scr[lo:lo + kv_chunk, :], preferred_element_type=F32)
            acc = pv if acc is None else acc + pv
            for n in range(c * per_chunk, min((c + 1) * per_chunk, LRU_BLOCKS)):
                lru_block(n)
        finish(acc)

    @pl.when(jnp.logical_not(safe))
    def _():
        s = lax.dot_general(qs, k_ref[...], nt, preferred_element_type=F32)
        p = jnp.exp2(s - jnp.max(s, axis=-1, keepdims=True)).astype(BF16)
        finish(jnp.dot(p, vext_scr[...], preferred_element_type=F32))
        lru_stage()
        for n in range(LRU_BLOCKS):
            lru_block(n)


def _mixer_group(q, k, v, xr, conv_w, conv_b, wcat, b_a, b_x, lam, group, reverse):
    b, s, w = xr.shape
    tq = min(TQ_ATTN, s)
    n_tiles = s // tq
    blocks8 = tq // SUBLANES
    gw = Q_GROUP * HEAD_DIM
    tile = (lambda i: n_tiles - 1 - i) if reverse else (lambda i: i)
    qspec = pl.BlockSpec((None, tq, gw), lambda bi, i: (bi, i, group))
    kvspec = pl.BlockSpec((None, s, HEAD_DIM), lambda bi, i: (bi, 0, group))
    main = pl.BlockSpec((None, tq, w), lambda bi, i: (bi, tile(i), 0))
    prev = pl.BlockSpec((None, SUBLANES, w),
                        lambda bi, i: (bi, jnp.maximum(tile(i) * blocks8 - 1, 0), 0))
    nxt = pl.BlockSpec((None, SUBLANES, w),
                       lambda bi, i: (bi, jnp.minimum((tile(i) + 1) * blocks8, s // SUBLANES - 1), 0))
    vec = _resident((1, w))
    return pl.pallas_call(
        functools.partial(_mixer_kernel, tq=tq, s_len=s, kv_chunk=min(KV_CHUNK, s), n_tiles=n_tiles,
                          reverse=reverse),
        grid=(b, n_tiles),
        in_specs=[qspec, kvspec, kvspec, main, prev, nxt, _resident((CONV_WIDTH, w)), vec,
                  _resident(wcat.shape), vec, vec, vec],
        out_specs=[pl.BlockSpec((None, tq, gw), lambda bi, i: (bi, i, 0)), main],
        out_shape=[jax.ShapeDtypeStruct((b, s, gw), BF16), jax.ShapeDtypeStruct((b, s, w), BF16)],
        scratch_shapes=[pltpu.VMEM((s, 2 * HEAD_DIM), BF16), pltpu.VMEM((SUBLANES, LANES), F32),
                        pltpu.VMEM((tq + 2 * SUBLANES, w), F32), pltpu.VMEM((SUBLANES, w), F32)],
        compiler_params=_params("parallel", "arbitrary"),
        name="mixer_bwd" if reverse else "mixer_fwd",
    )(q, k, v, xr, xr, xr, conv_w, conv_b.reshape(1, w), wcat, b_a.reshape(1, w), b_x.reshape(1, w),
      lam.reshape(1, w))


def _mixer(q, k, v, xr, conv_w, conv_b, w_a, b_a, w_x, b_x, lam):
    wcat = jnp.concatenate([w_a, w_x], axis=-1).astype(BF16)
    outs = [_mixer_group(q, k, v, xr, conv_w, conv_b, wcat[g], b_a[g], b_x[g], lam[g], g, g == 1)
            for g in range(N_KV_HEADS)]
    (attn0, h_fwd), (attn1, h_bwd) = outs
    return attn0, attn1, h_fwd, h_bwd


def _merge_kernel(attn0_ref, attn1_ref, hf_ref, hb_ref, gxg_ref, sg_ref, x_ref, mod_ref, woa_ref,
                  wol_ref, wout_ref, o_ref, *, d_model):
    gw = attn0_ref.shape[-1]
    ya = (jnp.dot(attn0_ref[...], woa_ref[:gw, :], preferred_element_type=F32)
          + jnp.dot(attn1_ref[...], woa_ref[gw:, :], preferred_element_type=F32))
    ylru = (hf_ref[...].astype(F32) + hb_ref[...].astype(F32)) * gxg_ref[...].astype(F32)
    yl = jnp.dot(ylru.astype(BF16), wol_ref[...], preferred_element_type=F32)
    sg = sg_ref[...].astype(F32)
    merged = sg[:, :d_model] * ya + sg[:, d_model:] * yl
    out = jnp.dot(merged.astype(BF16), wout_ref[...], preferred_element_type=F32)
    o_ref[...] = x_ref[...] + mod_ref[...][2:3] * out


def _merge(attn0, attn1, h_fwd, h_bwd, gxg, sg, x, mod, w_o_attn, w_o_lru, w_out):
    b, s, d = x.shape
    tm = min(TM_MERGE, s)
    row = lambda w: pl.BlockSpec((None, tm, w), lambda bi, i: (bi, i, 0))
    gw = attn0.shape[-1]
    return pl.pallas_call(
        functools.partial(_merge_kernel, d_model=d),
        grid=(b, s // tm),
        in_specs=[row(gw), row(gw), row(d), row(d), row(d), row(2 * d), row(d),
                  pl.BlockSpec((None, N_MOD, d), lambda bi, i: (bi, 0, 0)),
                  _resident(w_o_attn.shape), _resident(w_o_lru.shape), _resident(w_out.shape)],
        out_specs=row(d),
        out_shape=jax.ShapeDtypeStruct(x.shape, F32),
        compiler_params=_params("parallel", "parallel"),
        name="merge",
    )(attn0, attn1, h_fwd, h_bwd, gxg, sg, x, mod, w_o_attn, w_o_lru, w_out)


def _swiglu_chunks(hb, wg_ref, wu_ref, wd_ref):
    d_ff = wg_ref.shape[-1]
    acc = None
    for lo in range(0, d_ff, FF_CHUNK):
        hi = min(lo + FF_CHUNK, d_ff)
        g = jnp.dot(hb, wg_ref[:, lo:hi], preferred_element_type=F32)
        u = jnp.dot(hb, wu_ref[:, lo:hi], preferred_element_type=F32)
        act = (g * jax.nn.sigmoid(g) * u).astype(BF16)
        y = jnp.dot(act, wd_ref[lo:hi, :], preferred_element_type=F32)
        acc = y if acc is None else acc + y
    return acc


def _final_norm(x, gain):
    return x * lax.rsqrt(jnp.mean(x * x, axis=-1, keepdims=True) + NORM_EPS) * gain


def _ffn_kernel(x_ref, mod_ref, wg_ref, wu_ref, wd_ref, fg_ref, o_ref, *, final):
    x = x_ref[...]
    mod = mod_ref[...]
    hb = _modulated_norm(x, mod[3:4], mod[4:5]).astype(BF16)
    y = x + mod[5:6] * _swiglu_chunks(hb, wg_ref, wu_ref, wd_ref)
    o_ref[...] = _final_norm(y, fg_ref[...]) if final else y


def _dense_ffn(x, mod, w_gate, w_up, w_down, final_gain, final):
    b, s, d = x.shape
    tm = min(TM_FFN, s)
    row = pl.BlockSpec((None, tm, d), lambda bi, i: (bi, i, 0))
    return pl.pallas_call(
        functools.partial(_ffn_kernel, final=final),
        grid=(b, s // tm),
        in_specs=[row, pl.BlockSpec((None, N_MOD, d), lambda bi, i: (bi, 0, 0)),
                  _resident(w_gate.shape), _resident(w_up.shape), _resident(w_down.shape),
                  _resident((1, d))],
        out_specs=row,
        out_shape=jax.ShapeDtypeStruct(x.shape, F32),
        compiler_params=_params("parallel", "parallel"),
        name="dense_ffn",
    )(x, mod, w_gate, w_up, w_down, final_gain.reshape(1, d))


R_E0, R_E1, R_RANK0, R_RANK1, R_W0, R_W1 = range(6)


def _router_kernel(x_ref, mod_ref, rw_ref, rb_ref, hp_ref, info_ref, cnt_ref, carry_scr, *, tm):
    @pl.when((pl.program_id(0) == 0) & (pl.program_id(1) == 0))
    def _():
        carry_scr[...] = jnp.zeros_like(carry_scr)

    mod = mod_ref[...]
    h = _modulated_norm(x_ref[...], mod[3:4], mod[4:5])
    hp_ref[...] = h

    logits = jnp.dot(h, rw_ref[...], preferred_element_type=F32,
                     precision=lax.Precision.HIGHEST) + rb_ref[...]
    lane = lax.broadcasted_iota(jnp.int32, logits.shape, 1)
    m1 = jnp.max(logits, axis=-1, keepdims=True)
    e0 = jnp.min(jnp.where(logits == m1, lane, LANES), axis=-1, keepdims=True)
    rest = jnp.where(lane == e0, -jnp.inf, logits)
    m2 = jnp.max(rest, axis=-1, keepdims=True)
    e1 = jnp.min(jnp.where(rest == m2, lane, LANES), axis=-1, keepdims=True)
    t = jnp.exp(m2 - m1)
    w0 = 1.0 / (1.0 + t)
    w1 = t * w0

    hit0 = lane == e0
    hit1 = lane == e1
    onehot = jnp.where(hit0 | hit1, 1.0, 0.0)
    ri = lax.broadcasted_iota(jnp.int32, (tm, tm), 0)
    ci = lax.broadcasted_iota(jnp.int32, (tm, tm), 1)
    before = jnp.where(ci < ri, 1.0, 0.0).astype(BF16)
    seen = jnp.dot(before, onehot.astype(BF16), preferred_element_type=F32) + carry_scr[0:1, :]
    rank0 = jnp.sum(jnp.where(hit0, seen, 0.0), axis=-1, keepdims=True)
    rank1 = jnp.sum(jnp.where(hit1, seen, 0.0), axis=-1, keepdims=True)
    carry = carry_scr[0:1, :] + jnp.sum(onehot, axis=0, keepdims=True)
    carry_scr[0:1, :] = carry
    cnt_ref[...] = jnp.broadcast_to(carry, cnt_ref.shape)

    info = jnp.zeros(logits.shape, F32)
    for col, val in ((R_E0, e0.astype(F32)), (R_E1, e1.astype(F32)), (R_RANK0, rank0),
                     (R_RANK1, rank1), (R_W0, w0), (R_W1, w1)):
        info = jnp.where(lane == col, val, info)
    info_ref[...] = info


def _router(x, mod, router_w, router_b):
    b, s, d = x.shape
    tm = min(TM_ROUTER, s)
    n_e = router_w.shape[-1]
    rw = jnp.zeros((d, LANES), F32).at[:, :n_e].set(router_w)
    rb = jnp.full((1, LANES), -jnp.inf, F32).at[0, :n_e].set(router_b)
    nt = s // tm
    return pl.pallas_call(
        functools.partial(_router_kernel, tm=tm),
        grid=(b, nt),
        in_specs=[pl.BlockSpec((None, tm, d), lambda bi, i: (bi, i, 0)),
                  pl.BlockSpec((None, N_MOD, d), lambda bi, i: (bi, 0, 0)),
                  _resident(rw.shape), _resident(rb.shape)],
        out_specs=[pl.BlockSpec((tm, d), lambda bi, i: (bi * nt + i, 0)),
                   pl.BlockSpec((tm, LANES), lambda bi, i: (bi * nt + i, 0)),
                   _resident((SUBLANES, LANES))],
        out_shape=[jax.ShapeDtypeStruct((b * s, d), F32),
                   jax.ShapeDtypeStruct((b * s, LANES), F32),
                   jax.ShapeDtypeStruct((SUBLANES, LANES), F32)],
        scratch_shapes=[pltpu.VMEM((SUBLANES, LANES), F32)],
        compiler_params=_params("arbitrary", "arbitrary"),
        name="router",
    )(x, mod, rw, rb)


def _scatter_kernel(pos_ref, hp_ref, init_ref, xs_ref, sem, *, tm):
    del init_ref

    def copy(r, k):
        return pltpu.make_async_copy(hp_ref.at[pl.ds(r, 1)],
                                     xs_ref.at[pl.ds(pos_ref[0, 0, TOP_K * r + k], 1)], sem)

    def start(r, carry):
        for k in range(TOP_K):
            copy(r, k).start()
        return carry

    def wait(r, carry):
        for k in range(TOP_K):
            copy(r, k).wait()
        return carry

    lax.fori_loop(0, tm, start, 0, unroll=8)
    lax.fori_loop(0, tm, wait, 0, unroll=8)


def _scatter_rows(hp, pos, n_rows):
    n, w = hp.shape
    tm = min(TM_SCATTER, n)
    pos3 = pos.reshape(n // tm, 1, TOP_K * tm)
    return pl.pallas_call(
        functools.partial(_scatter_kernel, tm=tm),
        grid=(n // tm,),
        in_specs=[pl.BlockSpec((1, 1, TOP_K * tm), lambda i: (i, 0, 0), memory_space=pltpu.SMEM),
                  pl.BlockSpec((tm, w), lambda i: (i, 0)),
                  pl.BlockSpec(memory_space=pl.ANY)],
        out_specs=pl.BlockSpec(memory_space=pl.ANY),
        out_shape=jax.ShapeDtypeStruct((n_rows, w), hp.dtype),
        scratch_shapes=[pltpu.SemaphoreType.DMA(())],
        input_output_aliases={2: 0},
        compiler_params=_params("arbitrary"),
        name="moe_scatter",
    )(pos3, hp, jnp.zeros((n_rows, w), hp.dtype))


def _expert_kernel(te_ref, nu_ref, xs_ref, wg_ref, wu_ref, wd_ref, y_ref):
    del te_ref
    i = pl.program_id(0)

    @pl.when(i < nu_ref[0])
    def _():
        y_ref[...] = _swiglu_chunks(xs_ref[...].astype(BF16), wg_ref, wu_ref, wd_ref)

    @pl.when(i >= nu_ref[0])
    def _():
        y_ref[...] = jnp.zeros_like(y_ref)


def _grouped_experts(xs, tile_expert, n_used, w_gate, w_up, w_down):
    n_rows, d = xs.shape
    tm = TM_EXPERT
    ff = w_gate.shape[-1]
    grid_spec = pltpu.PrefetchScalarGridSpec(
        num_scalar_prefetch=2,
        grid=(n_rows // tm,),
        in_specs=[pl.BlockSpec((tm, d), lambda i, te, nu: (i, 0)),
                  pl.BlockSpec((None, d, ff), lambda i, te, nu: (te[i], 0, 0)),
                  pl.BlockSpec((None, d, ff), lambda i, te, nu: (te[i], 0, 0)),
                  pl.BlockSpec((None, ff, d), lambda i, te, nu: (te[i], 0, 0))],
        out_specs=pl.BlockSpec((tm, d), lambda i, te, nu: (i, 0)),
    )
    return pl.pallas_call(
        _expert_kernel,
        grid_spec=grid_spec,
        out_shape=jax.ShapeDtypeStruct((n_rows, d), F32),
        compiler_params=pltpu.CompilerParams(dimension_semantics=("arbitrary",),
                                             vmem_limit_bytes=60 * 1024 * 1024),
        name="moe_experts",
    )(tile_expert, n_used, xs, w_gate, w_up, w_down)


def _combine_kernel(pos_ref, y_ref, x_ref, mod_ref, info_ref, fg_ref, o_ref, buf, sem, *, tm, final):
    def copy(r, k):
        return pltpu.make_async_copy(y_ref.at[pl.ds(pos_ref[0, 0, TOP_K * r + k], 1)],
                                     buf.at[k, pl.ds(r, 1)], sem)

    def start(r, carry):
        for k in range(TOP_K):
            copy(r, k).start()
        return carry

    def wait(r, carry):
        for k in range(TOP_K):
            copy(r, k).wait()
        return carry

    lax.fori_loop(0, tm, start, 0, unroll=8)
    lax.fori_loop(0, tm, wait, 0, unroll=8)
    info = info_ref[...]
    w0 = info[:, R_W0:R_W0 + 1]
    w1 = info[:, R_W1:R_W1 + 1]
    ff = w0 * buf[0] + w1 * buf[1]
    y = x_ref[...] + mod_ref[...][5:6] * ff
    o_ref[...] = _final_norm(y, fg_ref[...]) if final else y


def _combine(y_sorted, pos, x, mod, info, final_gain, final):
    b, s, d = x.shape
    tm = min(TM_COMBINE, s)
    nt = s // tm
    pos3 = pos.reshape(b * nt, 1, TOP_K * tm)
    return pl.pallas_call(
        functools.partial(_combine_kernel, tm=tm, final=final),
        grid=(b, nt),
        in_specs=[pl.BlockSpec((1, 1, TOP_K * tm), lambda bi, i: (bi * nt + i, 0, 0),
                               memory_space=pltpu.SMEM),
                  pl.BlockSpec(memory_space=pl.ANY),
                  pl.BlockSpec((None, tm, d), lambda bi, i: (bi, i, 0)),
                  pl.BlockSpec((None, N_MOD, d), lambda bi, i: (bi, 0, 0)),
                  pl.BlockSpec((tm, LANES), lambda bi, i: (bi * nt + i, 0)),
                  _resident((1, d))],
        out_specs=pl.BlockSpec((None, tm, d), lambda bi, i: (bi, i, 0)),
        out_shape=jax.ShapeDtypeStruct(x.shape, F32),
        scratch_shapes=[pltpu.VMEM((TOP_K, tm, d), F32), pltpu.SemaphoreType.DMA(())],
        compiler_params=_params("arbitrary", "arbitrary"),
        name="moe_combine",
    )(pos3, y_sorted, x, mod, info, final_gain.reshape(1, d))


def _moe_ffn(x, mod, router_w, router_b, w_gate, w_up, w_down, final_gain, final):
    b, s, d = x.shape
    n = b * s
    n_e = router_w.shape[-1]
    tm = TM_EXPERT
    hp, info, cnt = _router(x, mod, router_w, router_b)

    counts = cnt[0, :n_e].astype(jnp.int32)
    tiles = (counts + tm - 1) // tm
    tile_end = jnp.cumsum(tiles)
    row_start = (tile_end - tiles) * tm
    n_tiles = (TOP_K * n) // tm + n_e
    tile_expert = jnp.minimum(
        jnp.sum(jnp.arange(n_tiles, dtype=jnp.int32)[:, None] >= tile_end[None, :], axis=1),
        n_e - 1).astype(jnp.int32)
    n_used = tile_end[-1:].astype(jnp.int32)
    experts = info[:, R_E0:R_E1 + 1].astype(jnp.int32)
    ranks = info[:, R_RANK0:R_RANK1 + 1].astype(jnp.int32)
    pos = (row_start[experts] + ranks).reshape(-1)

    xs = _scatter_rows(hp, pos, n_tiles * tm)
    y_sorted = _grouped_experts(xs, tile_expert, n_used, w_gate, w_up, w_down)
    return _combine(y_sorted, pos, x, mod, info, final_gain, final)


def _rope_tables(seq_len):
    pos = jnp.arange(seq_len, dtype=jnp.int32)
    axis_pos = jnp.stack([pos // GRID_W, pos % GRID_W], axis=1).astype(F32)
    n_freq = HEAD_DIM // 4
    inv_freq = jnp.exp(-math.log(ROPE_THETA) * (2.0 * jnp.arange(n_freq, dtype=F32) / (HEAD_DIM // 2)))
    ang = axis_pos[:, :, None] * inv_freq[None, None, :]
    cos = jnp.cos(ang)
    sin = jnp.sin(ang)
    cos_t = jnp.concatenate([cos, cos], axis=-1).reshape(seq_len, HEAD_DIM)
    sin_t = jnp.concatenate([-sin, sin], axis=-1).reshape(seq_len, HEAD_DIM)
    return cos_t, sin_t


def kernel(x, c, w_mod, b_mod, w_in, q_norm_gain, k_norm_gain, conv_w, conv_b, lru_w_a, lru_b_a,
           lru_w_x, lru_b_x, lru_lambda, w_o_attn, w_o_lru, w_out, ffn_w_gate, ffn_w_up,
           ffn_w_down, router_w, router_b, moe_w_gate, moe_w_up, moe_w_down, final_gain):
    b, s, d = x.shape
    depth = w_in.shape[0]
    cos, sin = _rope_tables(s)
    mods = _modulation(c, w_mod, b_mod).reshape(depth, b, N_MOD, d)
    for l in range(depth):
        mod = mods[l]
        q, k, v, xr, gxg, sg = _inproj(x, mod, w_in[l].astype(BF16), q_norm_gain[l], k_norm_gain[l],
                                       cos, sin)
        attn0, attn1, h_fwd, h_bwd = _mixer(q, k, v, xr, conv_w[l], conv_b[l], lru_w_a[l], lru_b_a[l],
                                            lru_w_x[l], lru_b_x[l], lru_lambda[l])
        x = _merge(attn0, attn1, h_fwd, h_bwd, gxg, sg, x, mod, w_o_attn[l].astype(BF16),
                   w_o_lru[l].astype(BF16), w_out[l].astype(BF16))
        final = l == depth - 1
        j = l // 2
        if l % 2 == 0:
            x = _dense_ffn(x, mod, ffn_w_gate[j].astype(BF16), ffn_w_up[j].astype(BF16),
                           ffn_w_down[j].astype(BF16), final_gain, final)
        else:
            x = _moe_ffn(x, mod, router_w[j], router_b[j], moe_w_gate[j].astype(BF16),
                         moe_w_up[j].astype(BF16), moe_w_down[j].astype(BF16), final_gain, final)
    return x
```

```python
import functools
import math

import jax
import jax.numpy as jnp
from jax import lax
from jax.experimental import pallas as pl
from jax.experimental.pallas import tpu as pltpu

HEAD_DIM = 128
N_Q_HEADS = 8
N_KV_HEADS = 2
Q_GROUP = N_Q_HEADS // N_KV_HEADS
GRID_W = 64
ROPE_THETA = 10000.0
ROPE_PAIR_SHIFT = HEAD_DIM // 4
LRU_BLOCKS = 8
LRU_C = 8.0
CONV_WIDTH = 4
N_MOD = 6
TOP_K = 2
NORM_EPS = 1e-6
LOG2_E = math.log2(math.e)
SOFTMAX_SAFE_LOG2_SHIFT = 55.0
BF16_ROUNDING_MARGIN = 1.0 + 2.0 ** -7

LANES = 128
SUBLANES = 8
VMEM_LIMIT_BYTES = 56 * 1024 * 1024

TM_INPROJ = 256
TQ_ATTN = 128
TM_MERGE = 512
TM_FFN = 256
TM_ROUTER = 256
TM_EXPERT = 256
TM_SCATTER = 256
TM_COMBINE = 256
FF_CHUNK = 512
KV_CHUNK = 512

BF16 = jnp.bfloat16
F32 = jnp.float32


def _params(*semantics):
    return pltpu.CompilerParams(dimension_semantics=semantics, vmem_limit_bytes=VMEM_LIMIT_BYTES)


def _resident(shape):
    zeros = (0,) * len(shape)
    return pl.BlockSpec(shape, lambda *_: zeros)


def _modulated_norm(x, shift, scale):
    ms = jnp.mean(x * x, axis=-1, keepdims=True)
    return x * lax.rsqrt(ms + NORM_EPS) * (1.0 + scale) + shift


def _mod_kernel(c_ref, w_ref, b_ref, o_ref):
    c = c_ref[...]
    act = c * jax.nn.sigmoid(c)
    o_ref[...] = jnp.dot(act, w_ref[...], preferred_element_type=F32,
                         precision=lax.Precision.HIGHEST) + b_ref[...]


def _modulation(c, w_mod, b_mod):
    depth, d, n = w_mod.shape
    b = c.shape[0]
    tn = 1536
    return pl.pallas_call(
        _mod_kernel,
        grid=(depth, n // tn),
        in_specs=[pl.BlockSpec((b, d), lambda l, j: (0, 0)),
                  pl.BlockSpec((None, d, tn), lambda l, j: (l, 0, j)),
                  pl.BlockSpec((None, 1, tn), lambda l, j: (l, 0, j))],
        out_specs=pl.BlockSpec((None, b, tn), lambda l, j: (l, 0, j)),
        out_shape=jax.ShapeDtypeStruct((depth, b, n), F32),
        compiler_params=_params("parallel", "parallel"),
        name="modulation",
    )(c, w_mod, b_mod.reshape(depth, 1, n))


def _inproj_kernel(x_ref, mod_ref, w_ref, qg_ref, kg_ref, cos_ref, sin_ref,
                   q_ref, k_ref, v_ref, xr_ref, gxg_ref, sg_ref, *, d_model):
    attn_w = N_Q_HEADS * HEAD_DIM
    kv_w = N_KV_HEADS * HEAD_DIM
    mod = mod_ref[...]
    hb = _modulated_norm(x_ref[...], mod[0:1], mod[1:2]).astype(BF16)

    cos = cos_ref[...]
    sin = sin_ref[...]
    lane = lax.broadcasted_iota(jnp.int32, cos.shape, 1)
    pair_first = (lane & ROPE_PAIR_SHIFT) == 0

    def norm_rope(y, gain, post_scale):
        r = lax.rsqrt(jnp.mean(y * y, axis=-1, keepdims=True) + NORM_EPS)
        yn = y * r * gain
        partner = jnp.where(pair_first,
                            pltpu.roll(yn, HEAD_DIM - ROPE_PAIR_SHIFT, 1),
                            pltpu.roll(yn, ROPE_PAIR_SHIFT, 1))
        return (yn * cos + partner * sin) * post_scale

    def proj(lo, width):
        return jnp.dot(hb, w_ref[:, lo:lo + width], preferred_element_type=F32)

    qf = proj(0, attn_w)
    qg = qg_ref[...]
    for h in range(N_Q_HEADS):
        sl = slice(h * HEAD_DIM, (h + 1) * HEAD_DIM)
        q_ref[:, sl] = norm_rope(qf[:, sl], qg, HEAD_DIM ** -0.5 * LOG2_E).astype(BF16)
    kf = proj(attn_w, kv_w)
    kg = kg_ref[...]
    for h in range(N_KV_HEADS):
        sl = slice(h * HEAD_DIM, (h + 1) * HEAD_DIM)
        k_ref[:, sl] = norm_rope(kf[:, sl], kg, 1.0).astype(BF16)
    v_ref[...] = proj(attn_w + kv_w, kv_w).astype(BF16)
    lo = attn_w + 2 * kv_w
    xr_ref[...] = proj(lo, d_model)
    gxg_ref[...] = jax.nn.gelu(proj(lo + d_model, d_model), approximate=True).astype(BF16)
    sg_ref[...] = jax.nn.sigmoid(proj(lo + 2 * d_model, 2 * d_model)).astype(BF16)


def _inproj(x, mod, w_in, q_gain, k_gain, cos, sin):
    b, s, d = x.shape
    tm = min(TM_INPROJ, s)
    attn_w = N_Q_HEADS * HEAD_DIM
    kv_w = N_KV_HEADS * HEAD_DIM
    row = lambda w: pl.BlockSpec((None, tm, w), lambda bi, i: (bi, i, 0))
    tab = pl.BlockSpec((tm, HEAD_DIM), lambda bi, i: (i, 0))
    outs = [(attn_w, BF16), (kv_w, BF16), (kv_w, BF16), (d, F32), (d, BF16), (2 * d, BF16)]
    return pl.pallas_call(
        functools.partial(_inproj_kernel, d_model=d),
        grid=(b, s // tm),
        in_specs=[row(d),
                  pl.BlockSpec((None, N_MOD, d), lambda bi, i: (bi, 0, 0)),
                  _resident(w_in.shape), _resident((1, HEAD_DIM)), _resident((1, HEAD_DIM)),
                  tab, tab],
        out_specs=[row(w) for w, _ in outs],
        out_shape=[jax.ShapeDtypeStruct((b, s, w), dt) for w, dt in outs],
        compiler_params=_params("parallel", "parallel"),
        name="inproj",
    )(x, mod, w_in, q_gain.reshape(1, HEAD_DIM), k_gain.reshape(1, HEAD_DIM), cos, sin)


def _lru_stage(xr_ref, prev_ref, next_ref, xs_scr, *, ts, tile, n_tiles):
    xs_scr[0:SUBLANES, :] = jnp.where(tile > 0, prev_ref[...], 0.0)
    xs_scr[SUBLANES:SUBLANES + ts, :] = xr_ref[...]
    xs_scr[SUBLANES + ts:, :] = jnp.where(tile < n_tiles - 1, next_ref[...], 0.0)


def _lru_block(n, cw_ref, cb_ref, wcat_ref, ba_ref, bx_ref, lam_ref, h_ref, xs_scr, carry_scr,
               *, ts, reverse):
    bw = LANES
    sl = slice(n * bw, (n + 1) * bw)

    def tap(shift):
        return xs_scr[SUBLANES + shift:SUBLANES + shift + ts, sl]

    cw = cw_ref[:, sl]
    xb = (cw[0:1] * tap(-2) + cw[1:2] * tap(-1) + cw[2:3] * tap(0) + cw[3:4] * tap(1)
          + cb_ref[:, sl])
    lam = lam_ref[:, sl]
    neg_c_softplus = -LRU_C * (jnp.maximum(-lam, 0.0) + jnp.log1p(jnp.exp(-jnp.abs(lam))))
    g = jnp.dot(xb.astype(BF16), wcat_ref[n], preferred_element_type=F32)
    r = jax.nn.sigmoid(g[:, :bw] + ba_ref[:, sl])
    i = jax.nn.sigmoid(g[:, bw:] + bx_ref[:, sl])
    log_a = neg_c_softplus * r
    a_all = jnp.exp(log_a)
    th = jnp.tanh(log_a)
    u_all = jnp.exp(0.5 * jnp.log(-2.0 * th / (1.0 - th))) * (i * xb)

    row = lax.broadcasted_iota(jnp.int32, (SUBLANES, bw), 0)
    n_groups = ts // SUBLANES
    state = carry_scr[0:1, sl]
    states = [None] * n_groups
    for v in (range(n_groups - 1, -1, -1) if reverse else range(n_groups)):
        rows = slice(v * SUBLANES, (v + 1) * SUBLANES)
        a = a_all[rows]
        u = u_all[rows]
        for d in (1, 2, 4):
            keep = (row < SUBLANES - d) if reverse else (row >= d)
            shift = (SUBLANES - d) if reverse else d
            a_prev = jnp.where(keep, pltpu.roll(a, shift, 0), 1.0)
            u_prev = jnp.where(keep, pltpu.roll(u, shift, 0), 0.0)
            u = a * u_prev + u
            a = a * a_prev
        h = u + a * state
        states[v] = h
        state = h[0:1] if reverse else h[SUBLANES - 1:SUBLANES]
    carry_scr[0:1, sl] = state
    h_ref[:, sl] = jnp.concatenate(states, axis=0).astype(BF16)


def _mixer_kernel(q_ref, k_ref, v_ref, xr_ref, prev_ref, next_ref, cw_ref, cb_ref, wcat_ref, ba_ref,
                  bx_ref, lam_ref, qg_ref, o_ref, h_ref, vext_scr, shift_scr, xs_scr, carry_scr,
                  *, tq, s_len, kv_chunk, n_tiles, reverse):
    nt = (((1,), (1,)), ((), ()))
    i = pl.program_id(1)

    @pl.when(i == 0)
    def _():
        carry_scr[...] = jnp.zeros_like(carry_scr)
        vext_scr[:, :HEAD_DIM] = v_ref[...]
        vext_scr[:, HEAD_DIM:] = jnp.ones((s_len, HEAD_DIM), BF16)
        kf = k_ref[...].astype(F32)
        k_norm_max = jnp.max(jnp.sqrt(jnp.sum(kf * kf, axis=-1, keepdims=True)))
        q_norm_max = jnp.max(jnp.abs(qg_ref[...])) * (LOG2_E * BF16_ROUNDING_MARGIN)
        shift_scr[0] = q_norm_max * k_norm_max

    lru_stage = functools.partial(_lru_stage, xr_ref, prev_ref, next_ref, xs_scr, ts=tq,
                                  tile=(n_tiles - 1 - i) if reverse else i, n_tiles=n_tiles)
    lru_block = functools.partial(_lru_block, cw_ref=cw_ref, cb_ref=cb_ref, wcat_ref=wcat_ref,
                                  ba_ref=ba_ref, bx_ref=bx_ref, lam_ref=lam_ref, h_ref=h_ref,
                                  xs_scr=xs_scr, carry_scr=carry_scr, ts=tq, reverse=reverse)

    q = q_ref[...]
    qs = jnp.concatenate([q[:, h * HEAD_DIM:(h + 1) * HEAD_DIM] for h in range(Q_GROUP)], axis=0)
    shift = shift_scr[0]
    safe = shift <= SOFTMAX_SAFE_LOG2_SHIFT

    def finish(acc):
        o = acc[:, :HEAD_DIM] / acc[:, HEAD_DIM:]
        for h in range(Q_GROUP):
            o_ref[:, h * HEAD_DIM:(h + 1) * HEAD_DIM] = o[h * tq:(h + 1) * tq].astype(BF16)

    @pl.when(safe)
    def _():
        lru_stage()
        chunks = list(range(0, s_len, kv_chunk))
        per_chunk = -(-LRU_BLOCKS // len(chunks))
        acc = None
        for c, lo in enumerate(chunks):
            s = lax.dot_general(qs, k_ref[lo:lo + kv_chunk, :], nt, preferred_element_type=F32)
            p = jnp.exp2(s - shift).astype(BF16)
            pv = jnp.dot(p, vext_scr[lo:lo + kv_chunk, :], preferred_element_type=F32)
            acc = pv if acc is None else acc + pv
            for n in range(c * per_chunk, min((c + 1) * per_chunk, LRU_BLOCKS)):
                lru_block(n)
        finish(acc)

    @pl.when(jnp.logical_not(safe))
    def _():
        s = lax.dot_general(qs, k_ref[...], nt, preferred_element_type=F32)
        p = jnp.exp2(s - jnp.max(s, axis=-1, keepdims=True)).astype(BF16)
        finish(jnp.dot(p, vext_scr[...], preferred_element_type=F32))
        lru_stage()
        for n in range(LRU_BLOCKS):
            lru_block(n)


def _mixer_group(q, k, v, xr, conv_w, conv_b, wcat, b_a, b_x, lam, q_gain, group, reverse):
    b, s, w = xr.shape
    tq = min(TQ_ATTN, s)
    n_tiles = s // tq
    blocks8 = tq // SUBLANES
    gw = Q_GROUP * HEAD_DIM
    tile = (lambda i: n_tiles - 1 - i) if reverse else (lambda i: i)
    qspec = pl.BlockSpec((None, tq, gw), lambda bi, i: (bi, i, group))
    kvspec = pl.BlockSpec((None, s, HEAD_DIM), lambda bi, i: (bi, 0, group))
    main = pl.BlockSpec((None, tq, w), lambda bi, i: (bi, tile(i), 0))
    prev = pl.BlockSpec((None, SUBLANES, w),
                        lambda bi, i: (bi, jnp.maximum(tile(i) * blocks8 - 1, 0), 0))
    nxt = pl.BlockSpec((None, SUBLANES, w),
                       lambda bi, i: (bi, jnp.minimum((tile(i) + 1) * blocks8, s // SUBLANES - 1), 0))
    vec = _resident((1, w))
    return pl.pallas_call(
        functools.partial(_mixer_kernel, tq=tq, s_len=s, kv_chunk=min(KV_CHUNK, s), n_tiles=n_tiles,
                          reverse=reverse),
        grid=(b, n_tiles),
        in_specs=[qspec, kvspec, kvspec, main, prev, nxt, _resident((CONV_WIDTH, w)), vec,
                  _resident(wcat.shape), vec, vec, vec, _resident((1, HEAD_DIM))],
        out_specs=[pl.BlockSpec((None, tq, gw), lambda bi, i: (bi, i, 0)), main],
        out_shape=[jax.ShapeDtypeStruct((b, s, gw), BF16), jax.ShapeDtypeStruct((b, s, w), BF16)],
        scratch_shapes=[pltpu.VMEM((s, 2 * HEAD_DIM), BF16), pltpu.SMEM((1,), F32),
                        pltpu.VMEM((tq + 2 * SUBLANES, w), F32), pltpu.VMEM((SUBLANES, w), F32)],
        compiler_params=_params("parallel", "arbitrary"),
        name="mixer_bwd" if reverse else "mixer_fwd",
    )(q, k, v, xr, xr, xr, conv_w, conv_b.reshape(1, w), wcat, b_a.reshape(1, w), b_x.reshape(1, w),
      lam.reshape(1, w), q_gain.reshape(1, HEAD_DIM))


def _mixer(q, k, v, xr, conv_w, conv_b, w_a, b_a, w_x, b_x, lam, q_gain):
    wcat = jnp.concatenate([w_a, w_x], axis=-1).astype(BF16)
    outs = [_mixer_group(q, k, v, xr, conv_w, conv_b, wcat[g], b_a[g], b_x[g], lam[g], q_gain, g,
                         g == 1)
            for g in range(N_KV_HEADS)]
    (attn0, h_fwd), (attn1, h_bwd) = outs
    return attn0, attn1, h_fwd, h_bwd


def _merge_kernel(attn0_ref, attn1_ref, hf_ref, hb_ref, gxg_ref, sg_ref, x_ref, mod_ref, woa_ref,
                  wol_ref, wout_ref, o_ref, *, d_model):
    gw = attn0_ref.shape[-1]
    ya = (jnp.dot(attn0_ref[...], woa_ref[:gw, :], preferred_element_type=F32)
          + jnp.dot(attn1_ref[...], woa_ref[gw:, :], preferred_element_type=F32))
    ylru = (hf_ref[...].astype(F32) + hb_ref[...].astype(F32)) * gxg_ref[...].astype(F32)
    yl = jnp.dot(ylru.astype(BF16), wol_ref[...], preferred_element_type=F32)
    sg = sg_ref[...].astype(F32)
    merged = sg[:, :d_model] * ya + sg[:, d_model:] * yl
    out = jnp.dot(merged.astype(BF16), wout_ref[...], preferred_element_type=F32)
    o_ref[...] = x_ref[...] + mod_ref[...][2:3] * out


def _merge(attn0, attn1, h_fwd, h_bwd, gxg, sg, x, mod, w_o_attn, w_o_lru, w_out):
    b, s, d = x.shape
    tm = min(TM_MERGE, s)
    row = lambda w: pl.BlockSpec((None, tm, w), lambda bi, i: (bi, i, 0))
    gw = attn0.shape[-1]
    return pl.pallas_call(
        functools.partial(_merge_kernel, d_model=d),
        grid=(b, s // tm),
        in_specs=[row(gw), row(gw), row(d), row(d), row(d), row(2 * d), row(d),
                  pl.BlockSpec((None, N_MOD, d), lambda bi, i: (bi, 0, 0)),
                  _resident(w_o_attn.shape), _resident(w_o_lru.shape), _resident(w_out.shape)],
        out_specs=row(d),
        out_shape=jax.ShapeDtypeStruct(x.shape, F32),
        compiler_params=_params("parallel", "parallel"),
        name="merge",
    )(attn0, attn1, h_fwd, h_bwd, gxg, sg, x, mod, w_o_attn, w_o_lru, w_out)


def _swiglu_chunks(hb, wg_ref, wu_ref, wd_ref):
    d_ff = wg_ref.shape[-1]
    acc = None
    for lo in range(0, d_ff, FF_CHUNK):
        hi = min(lo + FF_CHUNK, d_ff)
        g = jnp.dot(hb, wg_ref[:, lo:hi], preferred_element_type=F32)
        u = jnp.dot(hb, wu_ref[:, lo:hi], preferred_element_type=F32)
        act = (g * jax.nn.sigmoid(g) * u).astype(BF16)
        y = jnp.dot(act, wd_ref[lo:hi, :], preferred_element_type=F32)
        acc = y if acc is None else acc + y
    return acc


def _final_norm(x, gain):
    return x * lax.rsqrt(jnp.mean(x * x, axis=-1, keepdims=True) + NORM_EPS) * gain


def _ffn_kernel(x_ref, mod_ref, wg_ref, wu_ref, wd_ref, fg_ref, o_ref, *, final):
    x = x_ref[...]
    mod = mod_ref[...]
    hb = _modulated_norm(x, mod[3:4], mod[4:5]).astype(BF16)
    y = x + mod[5:6] * _swiglu_chunks(hb, wg_ref, wu_ref, wd_ref)
    o_ref[...] = _final_norm(y, fg_ref[...]) if final else y


def _dense_ffn(x, mod, w_gate, w_up, w_down, final_gain, final):
    b, s, d = x.shape
    tm = min(TM_FFN, s)
    row = pl.BlockSpec((None, tm, d), lambda bi, i: (bi, i, 0))
    return pl.pallas_call(
        functools.partial(_ffn_kernel, final=final),
        grid=(b, s // tm),
        in_specs=[row, pl.BlockSpec((None, N_MOD, d), lambda bi, i: (bi, 0, 0)),
                  _resident(w_gate.shape), _resident(w_up.shape), _resident(w_down.shape),
                  _resident((1, d))],
        out_specs=row,
        out_shape=jax.ShapeDtypeStruct(x.shape, F32),
        compiler_params=_params("parallel", "parallel"),
        name="dense_ffn",
    )(x, mod, w_gate, w_up, w_down, final_gain.reshape(1, d))


R_E0, R_E1, R_RANK0, R_RANK1, R_W0, R_W1 = range(6)


def _router_kernel(x_ref, mod_ref, rw_ref, rb_ref, hp_ref, info_ref, cnt_ref, carry_scr, *, tm):
    @pl.when((pl.program_id(0) == 0) & (pl.program_id(1) == 0))
    def _():
        carry_scr[...] = jnp.zeros_like(carry_scr)

    mod = mod_ref[...]
    h = _modulated_norm(x_ref[...], mod[3:4], mod[4:5])
    hp_ref[...] = h

    logits = jnp.dot(h, rw_ref[...], preferred_element_type=F32,
                     precision=lax.Precision.HIGHEST) + rb_ref[...]
    lane = lax.broadcasted_iota(jnp.int32, logits.shape, 1)
    m1 = jnp.max(logits, axis=-1, keepdims=True)
    e0 = jnp.min(jnp.where(logits == m1, lane, LANES), axis=-1, keepdims=True)
    rest = jnp.where(lane == e0, -jnp.inf, logits)
    m2 = jnp.max(rest, axis=-1, keepdims=True)
    e1 = jnp.min(jnp.where(rest == m2, lane, LANES), axis=-1, keepdims=True)
    t = jnp.exp(m2 - m1)
    w0 = 1.0 / (1.0 + t)
    w1 = t * w0

    hit0 = lane == e0
    hit1 = lane == e1
    onehot = jnp.where(hit0 | hit1, 1.0, 0.0)
    ri = lax.broadcasted_iota(jnp.int32, (tm, tm), 0)
    ci = lax.broadcasted_iota(jnp.int32, (tm, tm), 1)
    before = jnp.where(ci < ri, 1.0, 0.0).astype(BF16)
    seen = jnp.dot(before, onehot.astype(BF16), preferred_element_type=F32) + carry_scr[0:1, :]
    rank0 = jnp.sum(jnp.where(hit0, seen, 0.0), axis=-1, keepdims=True)
    rank1 = jnp.sum(jnp.where(hit1, seen, 0.0), axis=-1, keepdims=True)
    carry = carry_scr[0:1, :] + jnp.sum(onehot, axis=0, keepdims=True)
    carry_scr[0:1, :] = carry
    cnt_ref[...] = jnp.broadcast_to(carry, cnt_ref.shape)

    info = jnp.zeros(logits.shape, F32)
    for col, val in ((R_E0, e0.astype(F32)), (R_E1, e1.astype(F32)), (R_RANK0, rank0),
                     (R_RANK1, rank1), (R_W0, w0), (R_W1, w1)):
        info = jnp.where(lane == col, val, info)
    info_ref[...] = info


def _router(x, mod, router_w, router_b):
    b, s, d = x.shape
    tm = min(TM_ROUTER, s)
    n_e = router_w.shape[-1]
    rw = jnp.zeros((d, LANES), F32).at[:, :n_e].set(router_w)
    rb = jnp.full((1, LANES), -jnp.inf, F32).at[0, :n_e].set(router_b)
    nt = s // tm
    return pl.pallas_call(
        functools.partial(_router_kernel, tm=tm),
        grid=(b, nt),
        in_specs=[pl.BlockSpec((None, tm, d), lambda bi, i: (bi, i, 0)),
                  pl.BlockSpec((None, N_MOD, d), lambda bi, i: (bi, 0, 0)),
                  _resident(rw.shape), _resident(rb.shape)],
        out_specs=[pl.BlockSpec((tm, d), lambda bi, i: (bi * nt + i, 0)),
                   pl.BlockSpec((tm, LANES), lambda bi, i: (bi * nt + i, 0)),
                   _resident((SUBLANES, LANES))],
        out_shape=[jax.ShapeDtypeStruct((b * s, d), F32),
                   jax.ShapeDtypeStruct((b * s, LANES), F32),
                   jax.ShapeDtypeStruct((SUBLANES, LANES), F32)],
        scratch_shapes=[pltpu.VMEM((SUBLANES, LANES), F32)],
        compiler_params=_params("arbitrary", "arbitrary"),
        name="router",
    )(x, mod, rw, rb)


def _scatter_kernel(pos_ref, hp_ref, init_ref, xs_ref, sem, *, tm):
    del init_ref

    def copy(r, k):
        return pltpu.make_async_copy(hp_ref.at[pl.ds(r, 1)],
                                     xs_ref.at[pl.ds(pos_ref[0, 0, TOP_K * r + k], 1)], sem)

    def start(r, carry):
        for k in range(TOP_K):
            copy(r, k).start(priority=k)
        return carry

    def wait(r, carry):
        for k in range(TOP_K):
            copy(r, k).wait()
        return carry

    lax.fori_loop(0, tm, start, 0, unroll=8)
    lax.fori_loop(0, tm, wait, 0, unroll=8)


def _scatter_rows(hp, pos, n_rows):
    n, w = hp.shape
    tm = min(TM_SCATTER, n)
    pos3 = pos.reshape(n // tm, 1, TOP_K * tm)
    return pl.pallas_call(
        functools.partial(_scatter_kernel, tm=tm),
        grid=(n // tm,),
        in_specs=[pl.BlockSpec((1, 1, TOP_K * tm), lambda i: (i, 0, 0), memory_space=pltpu.SMEM),
                  pl.BlockSpec((tm, w), lambda i: (i, 0)),
                  pl.BlockSpec(memory_space=pl.ANY)],
        out_specs=pl.BlockSpec(memory_space=pl.ANY),
        out_shape=jax.ShapeDtypeStruct((n_rows, w), hp.dtype),
        scratch_shapes=[pltpu.SemaphoreType.DMA(())],
        input_output_aliases={2: 0},
        compiler_params=_params("arbitrary"),
        name="moe_scatter",
    )(pos3, hp, jnp.zeros((n_rows, w), hp.dtype))


def _expert_kernel(te_ref, nu_ref, xs_ref, wg_ref, wu_ref, wd_ref, y_ref):
    del te_ref
    i = pl.program_id(0)

    @pl.when(i < nu_ref[0])
    def _():
        y_ref[...] = _swiglu_chunks(xs_ref[...].astype(BF16), wg_ref, wu_ref, wd_ref)

    @pl.when(i >= nu_ref[0])
    def _():
        y_ref[...] = jnp.zeros_like(y_ref)


def _grouped_experts(xs, tile_expert, n_used, w_gate, w_up, w_down):
    n_rows, d = xs.shape
    tm = TM_EXPERT
    ff = w_gate.shape[-1]
    grid_spec = pltpu.PrefetchScalarGridSpec(
        num_scalar_prefetch=2,
        grid=(n_rows // tm,),
        in_specs=[pl.BlockSpec((tm, d), lambda i, te, nu: (i, 0)),
                  pl.BlockSpec((None, d, ff), lambda i, te, nu: (te[i], 0, 0)),
                  pl.BlockSpec((None, d, ff), lambda i, te, nu: (te[i], 0, 0)),
                  pl.BlockSpec((None, ff, d), lambda i, te, nu: (te[i], 0, 0))],
        out_specs=pl.BlockSpec((tm, d), lambda i, te, nu: (i, 0)),
    )
    return pl.pallas_call(
        _expert_kernel,
        grid_spec=grid_spec,
        out_shape=jax.ShapeDtypeStruct((n_rows, d), F32),
        compiler_params=pltpu.CompilerParams(dimension_semantics=("arbitrary",),
                                             vmem_limit_bytes=60 * 1024 * 1024),
        name="moe_experts",
    )(tile_expert, n_used, xs, w_gate, w_up, w_down)


def _combine_kernel(pos_ref, y_ref, x_ref, mod_ref, info_ref, fg_ref, o_ref, buf, sem, *, tm, final):
    def copy(r, k):
        return pltpu.make_async_copy(y_ref.at[pl.ds(pos_ref[0, 0, TOP_K * r + k], 1)],
                                     buf.at[k, pl.ds(r, 1)], sem)

    def start(r, carry):
        for k in range(TOP_K):
            copy(r, k).start(priority=k)
        return carry

    def wait(r, carry):
        for k in range(TOP_K):
            copy(r, k).wait()
        return carry

    lax.fori_loop(0, tm, start, 0, unroll=8)
    lax.fori_loop(0, tm, wait, 0, unroll=8)
    info = info_ref[...]
    w0 = info[:, R_W0:R_W0 + 1]
    w1 = info[:, R_W1:R_W1 + 1]
    ff = w0 * buf[0] + w1 * buf[1]
    y = x_ref[...] + mod_ref[...][5:6] * ff
    o_ref[...] = _final_norm(y, fg_ref[...]) if final else y


def _combine(y_sorted, pos, x, mod, info, final_gain, final):
    b, s, d = x.shape
    tm = min(TM_COMBINE, s)
    nt = s // tm
    pos3 = pos.reshape(b * nt, 1, TOP_K * tm)
    return pl.pallas_call(
        functools.partial(_combine_kernel, tm=tm, final=final),
        grid=(b, nt),
        in_specs=[pl.BlockSpec((1, 1, TOP_K * tm), lambda bi, i: (bi * nt + i, 0, 0),
                               memory_space=pltpu.SMEM),
                  pl.BlockSpec(memory_space=pl.ANY),
                  pl.BlockSpec((None, tm, d), lambda bi, i: (bi, i, 0)),
                  pl.BlockSpec((None, N_MOD, d), lambda bi, i: (bi, 0, 0)),
                  pl.BlockSpec((tm, LANES), lambda bi, i: (bi * nt + i, 0)),
                  _resident((1, d))],
        out_specs=pl.BlockSpec((None, tm, d), lambda bi, i: (bi, i, 0)),
        out_shape=jax.ShapeDtypeStruct(x.shape, F32),
        scratch_shapes=[pltpu.VMEM((TOP_K, tm, d), F32), pltpu.SemaphoreType.DMA(())],
        compiler_params=_params("arbitrary", "arbitrary"),
        name="moe_combine",
    )(pos3, y_sorted, x, mod, info, final_gain.reshape(1, d))


def _moe_ffn(x, mod, router_w, router_b, w_gate, w_up, w_down, final_gain, final):
    b, s, d = x.shape
    n = b * s
    n_e = router_w.shape[-1]
    tm = TM_EXPERT
    hp, info, cnt = _router(x, mod, router_w, router_b)

    counts = cnt[0, :n_e].astype(jnp.int32)
    tiles = (counts + tm - 1) // tm
    tile_end = jnp.cumsum(tiles)
    row_start = (tile_end - tiles) * tm
    n_tiles = (TOP_K * n) // tm + n_e
    tile_expert = jnp.minimum(
        jnp.sum(jnp.arange(n_tiles, dtype=jnp.int32)[:, None] >= tile_end[None, :], axis=1),
        n_e - 1).astype(jnp.int32)
    n_used = tile_end[-1:].astype(jnp.int32)
    experts = info[:, R_E0:R_E1 + 1].astype(jnp.int32)
    ranks = info[:, R_RANK0:R_RANK1 + 1].astype(jnp.int32)
    pos = (row_start[experts] + ranks).reshape(-1)

    xs = _scatter_rows(hp, pos, n_tiles * tm)
    y_sorted = _grouped_experts(xs, tile_expert, n_used, w_gate, w_up, w_down)
    return _combine(y_sorted, pos, x, mod, info, final_gain, final)


def _rope_tables(seq_len):
    pos = jnp.arange(seq_len, dtype=jnp.int32)
    axis_pos = jnp.stack([pos // GRID_W, pos % GRID_W], axis=1).astype(F32)
    n_freq = HEAD_DIM // 4
    inv_freq = jnp.exp(-math.log(ROPE_THETA) * (2.0 * jnp.arange(n_freq, dtype=F32) / (HEAD_DIM // 2)))
    ang = axis_pos[:, :, None] * inv_freq[None, None, :]
    cos = jnp.cos(ang)
    sin = jnp.sin(ang)
    cos_t = jnp.concatenate([cos, cos], axis=-1).reshape(seq_len, HEAD_DIM)
    sin_t = jnp.concatenate([-sin, sin], axis=-1).reshape(seq_len, HEAD_DIM)
    return cos_t, sin_t


def kernel(x, c, w_mod, b_mod, w_in, q_norm_gain, k_norm_gain, conv_w, conv_b, lru_w_a, lru_b_a,
           lru_w_x, lru_b_x, lru_lambda, w_o_attn, w_o_lru, w_out, ffn_w_gate, ffn_w_up,
           ffn_w_down, router_w, router_b, moe_w_gate, moe_w_up, moe_w_down, final_gain):
    b, s, d = x.shape
    depth = w_in.shape[0]
    cos, sin = _rope_tables(s)
    mods = _modulation(c, w_mod, b_mod).reshape(depth, b, N_MOD, d)
    for l in range(depth):
        mod = mods[l]
        q, k, v, xr, gxg, sg = _inproj(x, mod, w_in[l].astype(BF16), q_norm_gain[l], k_norm_gain[l],
                                       cos, sin)
        attn0, attn1, h_fwd, h_bwd = _mixer(q, k, v, xr, conv_w[l], conv_b[l], lru_w_a[l], lru_b_a[l],
                                            lru_w_x[l], lru_b_x[l], lru_lambda[l], q_norm_gain[l])
        x = _merge(attn0, attn1, h_fwd, h_bwd, gxg, sg, x, mod, w_o_attn[l].astype(BF16),
                   w_o_lru[l].astype(BF16), w_out[l].astype(BF16))
        final = l == depth - 1
        j = l // 2
        if l % 2 == 0:
            x = _dense_ffn(x, mod, ffn_w_gate[j].astype(BF16), ffn_w_up[j].astype(BF16),
                           ffn_w_down[j].astype(BF16), final_gain, final)
        else:
            x = _moe_ffn(x, mod, router_w[j], router_b[j], moe_w_gate[j].astype(BF16),
                         moe_w_up[j].astype(BF16), moe_w_down[j].astype(BF16), final_gain, final)
    return x
```

```python
import functools
import math

import jax
import jax.numpy as jnp
from jax import lax
from jax.experimental import pallas as pl
from jax.experimental.pallas import tpu as pltpu
from jax.experimental.pallas import tpu_sc as plsc

HEAD_DIM = 128
N_Q_HEADS = 8
N_KV_HEADS = 2
Q_GROUP = N_Q_HEADS // N_KV_HEADS
GRID_W = 64
ROPE_THETA = 10000.0
ROPE_PAIR_SHIFT = HEAD_DIM // 4
LRU_BLOCKS = 8
LRU_C = 8.0
CONV_WIDTH = 4
N_MOD = 6
TOP_K = 2
NORM_EPS = 1e-6
LOG2_E = math.log2(math.e)
SOFTMAX_SAFE_LOG2_SHIFT = 55.0
BF16_ROUNDING_MARGIN = 1.0 + 2.0 ** -7

LANES = 128
SUBLANES = 8
VMEM_LIMIT_BYTES = 56 * 1024 * 1024

TM_INPROJ = 256
TQ_ATTN = 256
TM_MERGE = 512
TM_FFN = 512
TM_ROUTER = 256
TM_EXPERT = 256
TM_SCATTER = 256
TM_COMBINE = 256
FF_CHUNK = 512
KV_CHUNK = 512

BF16 = jnp.bfloat16
F32 = jnp.float32


def _params(*semantics):
    return pltpu.CompilerParams(dimension_semantics=semantics, vmem_limit_bytes=VMEM_LIMIT_BYTES)


def _resident(shape):
    zeros = (0,) * len(shape)
    return pl.BlockSpec(shape, lambda *_: zeros, pipeline_mode=pl.Buffered(1))


def _modulated_norm(x, shift, scale):
    ms = jnp.mean(x * x, axis=-1, keepdims=True)
    return x * lax.rsqrt(ms + NORM_EPS) * (1.0 + scale) + shift


def _mod_kernel(c_ref, w_ref, b_ref, o_ref):
    c = c_ref[...]
    act = c * jax.nn.sigmoid(c)
    o_ref[...] = jnp.dot(act, w_ref[...], preferred_element_type=F32,
                         precision=lax.Precision.HIGHEST) + b_ref[...]


def _modulation(c, w_mod, b_mod):
    depth, d, n = w_mod.shape
    b = c.shape[0]
    tn = 1536
    return pl.pallas_call(
        _mod_kernel,
        grid=(depth, n // tn),
        in_specs=[pl.BlockSpec((b, d), lambda l, j: (0, 0)),
                  pl.BlockSpec((None, d, tn), lambda l, j: (l, 0, j)),
                  pl.BlockSpec((None, 1, tn), lambda l, j: (l, 0, j))],
        out_specs=pl.BlockSpec((None, b, tn), lambda l, j: (l, 0, j)),
        out_shape=jax.ShapeDtypeStruct((depth, b, n), F32),
        compiler_params=_params("parallel", "parallel"),
        name="modulation",
    )(c, w_mod, b_mod.reshape(depth, 1, n))


def _inproj_kernel(x_ref, mod_ref, w_ref, qg_ref, kg_ref, cos_ref, sin_ref,
                   q_ref, k_ref, v_ref, xr_ref, gxg_ref, sg_ref, *, d_model):
    attn_w = N_Q_HEADS * HEAD_DIM
    kv_w = N_KV_HEADS * HEAD_DIM
    mod = mod_ref[...]
    hb = _modulated_norm(x_ref[...], mod[0:1], mod[1:2]).astype(BF16)

    cos = cos_ref[...]
    sin = sin_ref[...]
    lane = lax.broadcasted_iota(jnp.int32, cos.shape, 1)
    pair_first = (lane & ROPE_PAIR_SHIFT) == 0

    def norm_rope(y, gain, post_scale):
        r = lax.rsqrt(jnp.mean(y * y, axis=-1, keepdims=True) + NORM_EPS)
        yn = y * r * gain
        partner = jnp.where(pair_first,
                            pltpu.roll(yn, HEAD_DIM - ROPE_PAIR_SHIFT, 1),
                            pltpu.roll(yn, ROPE_PAIR_SHIFT, 1))
        return (yn * cos + partner * sin) * post_scale

    def proj(lo, width):
        return jnp.dot(hb, w_ref[:, lo:lo + width], preferred_element_type=F32)

    qf = proj(0, attn_w)
    qg = qg_ref[...]
    for h in range(N_Q_HEADS):
        sl = slice(h * HEAD_DIM, (h + 1) * HEAD_DIM)
        q_ref[:, sl] = norm_rope(qf[:, sl], qg, HEAD_DIM ** -0.5 * LOG2_E).astype(BF16)
    kf = proj(attn_w, kv_w)
    kg = kg_ref[...]
    for h in range(N_KV_HEADS):
        sl = slice(h * HEAD_DIM, (h + 1) * HEAD_DIM)
        k_ref[:, sl] = norm_rope(kf[:, sl], kg, 1.0).astype(BF16)
    v_ref[...] = proj(attn_w + kv_w, kv_w).astype(BF16)
    lo = attn_w + 2 * kv_w
    xr_ref[...] = proj(lo, d_model)
    gxg_ref[...] = jax.nn.gelu(proj(lo + d_model, d_model), approximate=True).astype(BF16)
    sg_ref[...] = jax.nn.sigmoid(proj(lo + 2 * d_model, 2 * d_model)).astype(BF16)


def _inproj(x, mod, w_in, q_gain, k_gain, cos, sin):
    b, s, d = x.shape
    tm = min(TM_INPROJ, s)
    attn_w = N_Q_HEADS * HEAD_DIM
    kv_w = N_KV_HEADS * HEAD_DIM
    row = lambda w: pl.BlockSpec((None, tm, w), lambda bi, i: (bi, i, 0))
    tab = pl.BlockSpec((tm, HEAD_DIM), lambda bi, i: (i, 0))
    outs = [(attn_w, BF16), (kv_w, BF16), (kv_w, BF16), (d, F32), (d, BF16), (2 * d, BF16)]
    return pl.pallas_call(
        functools.partial(_inproj_kernel, d_model=d),
        grid=(b, s // tm),
        in_specs=[row(d),
                  pl.BlockSpec((None, N_MOD, d), lambda bi, i: (bi, 0, 0)),
                  _resident(w_in.shape), _resident((1, HEAD_DIM)), _resident((1, HEAD_DIM)),
                  tab, tab],
        out_specs=[row(w) for w, _ in outs],
        out_shape=[jax.ShapeDtypeStruct((b, s, w), dt) for w, dt in outs],
        compiler_params=_params("parallel", "parallel"),
        name="inproj",
    )(x, mod, w_in, q_gain.reshape(1, HEAD_DIM), k_gain.reshape(1, HEAD_DIM), cos, sin)


def _lru_stage(xr_ref, prev_ref, next_ref, xs_scr, *, ts, tile, n_tiles):
    xs_scr[0:SUBLANES, :] = jnp.where(tile > 0, prev_ref[...], 0.0)
    xs_scr[SUBLANES:SUBLANES + ts, :] = xr_ref[...]
    xs_scr[SUBLANES + ts:, :] = jnp.where(tile < n_tiles - 1, next_ref[...], 0.0)


def _lru_block(n, cw_ref, cb_ref, wcat_ref, ba_ref, bx_ref, lam_ref, h_ref, xs_scr, carry_scr,
               *, ts, reverse):
    bw = LANES
    sl = slice(n * bw, (n + 1) * bw)

    def tap(shift):
        return xs_scr[SUBLANES + shift:SUBLANES + shift + ts, sl]

    cw = cw_ref[:, sl]
    xb = (cw[0:1] * tap(-2) + cw[1:2] * tap(-1) + cw[2:3] * tap(0) + cw[3:4] * tap(1)
          + cb_ref[:, sl])
    lam = lam_ref[:, sl]
    neg_c_softplus = -LRU_C * (jnp.maximum(-lam, 0.0) + jnp.log1p(jnp.exp(-jnp.abs(lam))))
    g = jnp.dot(xb.astype(BF16), wcat_ref[n], preferred_element_type=F32)
    r = jax.nn.sigmoid(g[:, :bw] + ba_ref[:, sl])
    i = jax.nn.sigmoid(g[:, bw:] + bx_ref[:, sl])
    log_a = neg_c_softplus * r
    a_all = jnp.exp(log_a)
    th = jnp.tanh(log_a)
    u_all = jnp.exp(0.5 * jnp.log(-2.0 * th / (1.0 - th))) * (i * xb)

    row = lax.broadcasted_iota(jnp.int32, (SUBLANES, bw), 0)
    n_groups = ts // SUBLANES
    state = carry_scr[0:1, sl]
    states = [None] * n_groups
    for v in (range(n_groups - 1, -1, -1) if reverse else range(n_groups)):
        rows = slice(v * SUBLANES, (v + 1) * SUBLANES)
        a = a_all[rows]
        u = u_all[rows]
        for d in (1, 2, 4):
            keep = (row < SUBLANES - d) if reverse else (row >= d)
            shift = (SUBLANES - d) if reverse else d
            a_prev = jnp.where(keep, pltpu.roll(a, shift, 0), 1.0)
            u_prev = jnp.where(keep, pltpu.roll(u, shift, 0), 0.0)
            u = a * u_prev + u
            a = a * a_prev
        h = u + a * state
        states[v] = h
        state = h[0:1] if reverse else h[SUBLANES - 1:SUBLANES]
    carry_scr[0:1, sl] = state
    h_ref[:, sl] = jnp.concatenate(states, axis=0).astype(BF16)


def _mixer_kernel(q_ref, k_ref, v_ref, xr_ref, prev_ref, next_ref, cw_ref, cb_ref, wcat_ref, ba_ref,
                  bx_ref, lam_ref, qg_ref, o_ref, h_ref, vext_scr, shift_scr, xs_scr, carry_scr,
                  *, tq, s_len, kv_chunk, n_tiles, reverse):
    nt = (((1,), (1,)), ((), ()))
    i = pl.program_id(1)

    @pl.when(i == 0)
    def _():
        carry_scr[...] = jnp.zeros_like(carry_scr)
        vext_scr[:, :HEAD_DIM] = v_ref[...]
        vext_scr[:, HEAD_DIM:] = jnp.ones((s_len, HEAD_DIM), BF16)
        kf = k_ref[...].astype(F32)
        k_norm_max = jnp.max(jnp.sqrt(jnp.sum(kf * kf, axis=-1, keepdims=True)))
        q_norm_max = jnp.max(jnp.abs(qg_ref[...])) * (LOG2_E * BF16_ROUNDING_MARGIN)
        shift_scr[0] = q_norm_max * k_norm_max

    lru_stage = functools.partial(_lru_stage, xr_ref, prev_ref, next_ref, xs_scr, ts=tq,
                                  tile=(n_tiles - 1 - i) if reverse else i, n_tiles=n_tiles)
    lru_block = functools.partial(_lru_block, cw_ref=cw_ref, cb_ref=cb_ref, wcat_ref=wcat_ref,
                                  ba_ref=ba_ref, bx_ref=bx_ref, lam_ref=lam_ref, h_ref=h_ref,
                                  xs_scr=xs_scr, carry_scr=carry_scr, ts=tq, reverse=reverse)

    q = q_ref[...]
    qs = jnp.concatenate([q[:, h * HEAD_DIM:(h + 1) * HEAD_DIM] for h in range(Q_GROUP)], axis=0)
    shift = shift_scr[0]
    safe = shift <= SOFTMAX_SAFE_LOG2_SHIFT

    def finish(acc):
        o = acc[:, :HEAD_DIM] / acc[:, HEAD_DIM:]
        for h in range(Q_GROUP):
            o_ref[:, h * HEAD_DIM:(h + 1) * HEAD_DIM] = o[h * tq:(h + 1) * tq].astype(BF16)

    @pl.when(safe)
    def _():
        lru_stage()
        chunks = list(range(0, s_len, kv_chunk))
        per_chunk = -(-LRU_BLOCKS // len(chunks))
        acc = None
        for c, lo in enumerate(chunks):
            s = lax.dot_general(qs, k_ref[lo:lo + kv_chunk, :], nt, preferred_element_type=F32)
            p = jnp.exp2(s - shift).astype(BF16)
            pv = jnp.dot(p, vext_scr[lo:lo + kv_chunk, :], preferred_element_type=F32)
            acc = pv if acc is None else acc + pv
            for n in range(c * per_chunk, min((c + 1) * per_chunk, LRU_BLOCKS)):
                lru_block(n)
        finish(acc)

    @pl.when(jnp.logical_not(safe))
    def _():
        s = lax.dot_general(qs, k_ref[...], nt, preferred_element_type=F32)
        p = jnp.exp2(s - jnp.max(s, axis=-1, keepdims=True)).astype(BF16)
        finish(jnp.dot(p, vext_scr[...], preferred_element_type=F32))
        lru_stage()
        for n in range(LRU_BLOCKS):
            lru_block(n)


def _mixer_group(q, k, v, xr, conv_w, conv_b, wcat, b_a, b_x, lam, q_gain, group, reverse):
    b, s, w = xr.shape
    tq = min(TQ_ATTN, s)
    n_tiles = s // tq
    blocks8 = tq // SUBLANES
    gw = Q_GROUP * HEAD_DIM
    tile = (lambda i: n_tiles - 1 - i) if reverse else (lambda i: i)
    qspec = pl.BlockSpec((None, tq, gw), lambda bi, i: (bi, i, group))
    kvspec = pl.BlockSpec((None, s, HEAD_DIM), lambda bi, i: (bi, 0, group))
    main = pl.BlockSpec((None, tq, w), lambda bi, i: (bi, tile(i), 0))
    prev = pl.BlockSpec((None, SUBLANES, w),
                        lambda bi, i: (bi, jnp.maximum(tile(i) * blocks8 - 1, 0), 0))
    nxt = pl.BlockSpec((None, SUBLANES, w),
                       lambda bi, i: (bi, jnp.minimum((tile(i) + 1) * blocks8, s // SUBLANES - 1), 0))
    vec = _resident((1, w))
    return pl.pallas_call(
        functools.partial(_mixer_kernel, tq=tq, s_len=s, kv_chunk=min(KV_CHUNK, s), n_tiles=n_tiles,
                          reverse=reverse),
        grid=(b, n_tiles),
        in_specs=[qspec, kvspec, kvspec, main, prev, nxt, _resident((CONV_WIDTH, w)), vec,
                  _resident(wcat.shape), vec, vec, vec, _resident((1, HEAD_DIM))],
        out_specs=[pl.BlockSpec((None, tq, gw), lambda bi, i: (bi, i, 0)), main],
        out_shape=[jax.ShapeDtypeStruct((b, s, gw), BF16), jax.ShapeDtypeStruct((b, s, w), BF16)],
        scratch_shapes=[pltpu.VMEM((s, 2 * HEAD_DIM), BF16), pltpu.SMEM((1,), F32),
                        pltpu.VMEM((tq + 2 * SUBLANES, w), F32), pltpu.VMEM((SUBLANES, w), F32)],
        compiler_params=_params("parallel", "arbitrary"),
        name="mixer_bwd" if reverse else "mixer_fwd",
    )(q, k, v, xr, xr, xr, conv_w, conv_b.reshape(1, w), wcat, b_a.reshape(1, w), b_x.reshape(1, w),
      lam.reshape(1, w), q_gain.reshape(1, HEAD_DIM))


def _mixer(q, k, v, xr, conv_w, conv_b, w_a, b_a, w_x, b_x, lam, q_gain):
    wcat = jnp.concatenate([w_a, w_x], axis=-1).astype(BF16)
    outs = [_mixer_group(q, k, v, xr, conv_w, conv_b, wcat[g], b_a[g], b_x[g], lam[g], q_gain, g,
                         g == 1)
            for g in range(N_KV_HEADS)]
    (attn0, h_fwd), (attn1, h_bwd) = outs
    return attn0, attn1, h_fwd, h_bwd


def _merge_kernel(attn0_ref, attn1_ref, hf_ref, hb_ref, gxg_ref, sg_ref, x_ref, mod_ref, woa_ref,
                  wol_ref, wout_ref, o_ref, *, d_model):
    gw = attn0_ref.shape[-1]
    ya = (jnp.dot(attn0_ref[...], woa_ref[:gw, :], preferred_element_type=F32)
          + jnp.dot(attn1_ref[...], woa_ref[gw:, :], preferred_element_type=F32))
    ylru = (hf_ref[...].astype(F32) + hb_ref[...].astype(F32)) * gxg_ref[...].astype(F32)
    yl = jnp.dot(ylru.astype(BF16), wol_ref[...], preferred_element_type=F32)
    sg = sg_ref[...].astype(F32)
    merged = sg[:, :d_model] * ya + sg[:, d_model:] * yl
    out = jnp.dot(merged.astype(BF16), wout_ref[...], preferred_element_type=F32)
    o_ref[...] = x_ref[...] + mod_ref[...][2:3] * out


def _merge(attn0, attn1, h_fwd, h_bwd, gxg, sg, x, mod, w_o_attn, w_o_lru, w_out):
    b, s, d = x.shape
    tm = min(TM_MERGE, s)
    row = lambda w: pl.BlockSpec((None, tm, w), lambda bi, i: (bi, i, 0))
    gw = attn0.shape[-1]
    return pl.pallas_call(
        functools.partial(_merge_kernel, d_model=d),
        grid=(b, s // tm),
        in_specs=[row(gw), row(gw), row(d), row(d), row(d), row(2 * d), row(d),
                  pl.BlockSpec((None, N_MOD, d), lambda bi, i: (bi, 0, 0)),
                  _resident(w_o_attn.shape), _resident(w_o_lru.shape), _resident(w_out.shape)],
        out_specs=row(d),
        out_shape=jax.ShapeDtypeStruct(x.shape, F32),
        compiler_params=_params("parallel", "parallel"),
        name="merge",
    )(attn0, attn1, h_fwd, h_bwd, gxg, sg, x, mod, w_o_attn, w_o_lru, w_out)


def _swiglu_chunks(hb, wg_ref, wu_ref, wd_ref):
    d_ff = wg_ref.shape[-1]
    acc = None
    for lo in range(0, d_ff, FF_CHUNK):
        hi = min(lo + FF_CHUNK, d_ff)
        g = jnp.dot(hb, wg_ref[:, lo:hi], preferred_element_type=F32)
        u = jnp.dot(hb, wu_ref[:, lo:hi], preferred_element_type=F32)
        act = (g * jax.nn.sigmoid(g) * u).astype(BF16)
        y = jnp.dot(act, wd_ref[lo:hi, :], preferred_element_type=F32)
        acc = y if acc is None else acc + y
    return acc


def _final_norm(x, gain):
    return x * lax.rsqrt(jnp.mean(x * x, axis=-1, keepdims=True) + NORM_EPS) * gain


def _ffn_kernel(x_ref, mod_ref, wg_ref, wu_ref, wd_ref, fg_ref, o_ref, *, final):
    x = x_ref[...]
    mod = mod_ref[...]
    hb = _modulated_norm(x, mod[3:4], mod[4:5]).astype(BF16)
    y = x + mod[5:6] * _swiglu_chunks(hb, wg_ref, wu_ref, wd_ref)
    o_ref[...] = _final_norm(y, fg_ref[...]) if final else y


def _dense_ffn(x, mod, w_gate, w_up, w_down, final_gain, final):
    b, s, d = x.shape
    tm = min(TM_FFN, s)
    row = pl.BlockSpec((None, tm, d), lambda bi, i: (bi, i, 0))
    return pl.pallas_call(
        functools.partial(_ffn_kernel, final=final),
        grid=(b, s // tm),
        in_specs=[row, pl.BlockSpec((None, N_MOD, d), lambda bi, i: (bi, 0, 0)),
                  _resident(w_gate.shape), _resident(w_up.shape), _resident(w_down.shape),
                  _resident((1, d))],
        out_specs=row,
        out_shape=jax.ShapeDtypeStruct(x.shape, F32),
        compiler_params=_params("parallel", "parallel"),
        name="dense_ffn",
    )(x, mod, w_gate, w_up, w_down, final_gain.reshape(1, d))


R_E0, R_E1, R_RANK0, R_RANK1, R_W0, R_W1 = range(6)


def _router_kernel(x_ref, mod_ref, rw_ref, rb_ref, hp_ref, info_ref, cnt_ref, carry_scr, *, tm):
    @pl.when((pl.program_id(0) == 0) & (pl.program_id(1) == 0))
    def _():
        carry_scr[...] = jnp.zeros_like(carry_scr)

    mod = mod_ref[...]
    h = _modulated_norm(x_ref[...], mod[3:4], mod[4:5])
    hp_ref[...] = h

    logits = jnp.dot(h, rw_ref[...], preferred_element_type=F32,
                     precision=lax.Precision.HIGHEST) + rb_ref[...]
    lane = lax.broadcasted_iota(jnp.int32, logits.shape, 1)
    m1 = jnp.max(logits, axis=-1, keepdims=True)
    e0 = jnp.min(jnp.where(logits == m1, lane, LANES), axis=-1, keepdims=True)
    rest = jnp.where(lane == e0, -jnp.inf, logits)
    m2 = jnp.max(rest, axis=-1, keepdims=True)
    e1 = jnp.min(jnp.where(rest == m2, lane, LANES), axis=-1, keepdims=True)
    t = jnp.exp(m2 - m1)
    w0 = 1.0 / (1.0 + t)
    w1 = t * w0

    hit0 = lane == e0
    hit1 = lane == e1
    onehot = jnp.where(hit0 | hit1, 1.0, 0.0)
    ri = lax.broadcasted_iota(jnp.int32, (tm, tm), 0)
    ci = lax.broadcasted_iota(jnp.int32, (tm, tm), 1)
    before = jnp.where(ci < ri, 1.0, 0.0).astype(BF16)
    seen = jnp.dot(before, onehot.astype(BF16), preferred_element_type=F32) + carry_scr[0:1, :]
    rank0 = jnp.sum(jnp.where(hit0, seen, 0.0), axis=-1, keepdims=True)
    rank1 = jnp.sum(jnp.where(hit1, seen, 0.0), axis=-1, keepdims=True)
    carry = carry_scr[0:1, :] + jnp.sum(onehot, axis=0, keepdims=True)
    carry_scr[0:1, :] = carry
    cnt_ref[...] = jnp.broadcast_to(carry, cnt_ref.shape)

    info = jnp.zeros(logits.shape, F32)
    for col, val in ((R_E0, e0.astype(F32)), (R_E1, e1.astype(F32)), (R_RANK0, rank0),
                     (R_RANK1, rank1), (R_W0, w0), (R_W1, w1)):
        info = jnp.where(lane == col, val, info)
    info_ref[...] = info


def _router(x, mod, router_w, router_b):
    b, s, d = x.shape
    tm = min(TM_ROUTER, s)
    n_e = router_w.shape[-1]
    rw = jnp.zeros((d, LANES), F32).at[:, :n_e].set(router_w)
    rb = jnp.full((1, LANES), -jnp.inf, F32).at[0, :n_e].set(router_b)
    nt = s // tm
    return pl.pallas_call(
        functools.partial(_router_kernel, tm=tm),
        grid=(b, nt),
        in_specs=[pl.BlockSpec((None, tm, d), lambda bi, i: (bi, i, 0)),
                  pl.BlockSpec((None, N_MOD, d), lambda bi, i: (bi, 0, 0)),
                  _resident(rw.shape), _resident(rb.shape)],
        out_specs=[pl.BlockSpec((tm, d), lambda bi, i: (bi * nt + i, 0)),
                   pl.BlockSpec((tm, LANES), lambda bi, i: (bi * nt + i, 0)),
                   _resident((SUBLANES, LANES))],
        out_shape=[jax.ShapeDtypeStruct((b * s, d), F32),
                   jax.ShapeDtypeStruct((b * s, LANES), F32),
                   jax.ShapeDtypeStruct((SUBLANES, LANES), F32)],
        scratch_shapes=[pltpu.VMEM((SUBLANES, LANES), F32)],
        compiler_params=_params("arbitrary", "arbitrary"),
        name="router",
    )(x, mod, rw, rb)


def _scatter_kernel(pos_ref, hp_ref, init_ref, xs_ref, sem, *, tm):
    del init_ref

    def copy(r, k):
        return pltpu.make_async_copy(hp_ref.at[pl.ds(r, 1)],
                                     xs_ref.at[pl.ds(pos_ref[0, 0, TOP_K * r + k], 1)], sem)

    def start(r, carry):
        for k in range(TOP_K):
            copy(r, k).start(priority=k)
        return carry

    def wait(r, carry):
        for k in range(TOP_K):
            copy(r, k).wait()
        return carry

    lax.fori_loop(0, tm, start, 0, unroll=8)
    lax.fori_loop(0, tm, wait, 0, unroll=8)


def _scatter_rows(hp, pos, n_rows):
    n, w = hp.shape
    tm = min(TM_SCATTER, n)
    pos3 = pos.reshape(n // tm, 1, TOP_K * tm)
    return pl.pallas_call(
        functools.partial(_scatter_kernel, tm=tm),
        grid=(n // tm,),
        in_specs=[pl.BlockSpec((1, 1, TOP_K * tm), lambda i: (i, 0, 0), memory_space=pltpu.SMEM),
                  pl.BlockSpec((tm, w), lambda i: (i, 0)),
                  pl.BlockSpec(memory_space=pl.ANY)],
        out_specs=pl.BlockSpec(memory_space=pl.ANY),
        out_shape=jax.ShapeDtypeStruct((n_rows, w), hp.dtype),
        scratch_shapes=[pltpu.SemaphoreType.DMA(())],
        input_output_aliases={2: 0},
        compiler_params=_params("arbitrary"),
        name="moe_scatter",
    )(pos3, hp, jnp.zeros((n_rows, w), hp.dtype))


def _expert_kernel(te_ref, nu_ref, xs_ref, wg_ref, wu_ref, wd_ref, y_ref):
    del te_ref
    i = pl.program_id(0)

    @pl.when(i < nu_ref[0])
    def _():
        y_ref[...] = _swiglu_chunks(xs_ref[...].astype(BF16), wg_ref, wu_ref, wd_ref)

    @pl.when(i >= nu_ref[0])
    def _():
        y_ref[...] = jnp.zeros_like(y_ref)


def _grouped_experts(xs, tile_expert, n_used, w_gate, w_up, w_down):
    n_rows, d = xs.shape
    tm = TM_EXPERT
    ff = w_gate.shape[-1]
    grid_spec = pltpu.PrefetchScalarGridSpec(
        num_scalar_prefetch=2,
        grid=(n_rows // tm,),
        in_specs=[pl.BlockSpec((tm, d), lambda i, te, nu: (i, 0)),
                  pl.BlockSpec((None, d, ff), lambda i, te, nu: (te[i], 0, 0)),
                  pl.BlockSpec((None, d, ff), lambda i, te, nu: (te[i], 0, 0)),
                  pl.BlockSpec((None, ff, d), lambda i, te, nu: (te[i], 0, 0))],
        out_specs=pl.BlockSpec((tm, d), lambda i, te, nu: (i, 0)),
    )
    return pl.pallas_call(
        _expert_kernel,
        grid_spec=grid_spec,
        out_shape=jax.ShapeDtypeStruct((n_rows, d), F32),
        compiler_params=pltpu.CompilerParams(dimension_semantics=("arbitrary",),
                                             vmem_limit_bytes=60 * 1024 * 1024),
        name="moe_experts",
    )(tile_expert, n_used, xs, w_gate, w_up, w_down)


SC_GATHER_WINDOW = 32


def _gather_rows(table, idx):
    m = idx.shape[0]
    d = table.shape[1]
    mesh = plsc.VectorSubcoreMesh(core_axis_name="core", subcore_axis_name="subcore")

    @functools.partial(pl.kernel, out_type=jax.ShapeDtypeStruct((m, d), table.dtype), mesh=mesh,
                       scratch_types=[])
    def gather(table_hbm, idx_hbm, out_hbm):
        def body(idx_vmem, out_vmem):
            pltpu.sync_copy(table_hbm.at[idx_vmem.at[0, pl.ds(0, SC_GATHER_WINDOW)]], out_vmem)

        pltpu.emit_pipeline(
            body,
            grid=(m // SC_GATHER_WINDOW,),
            in_specs=[pl.BlockSpec((1, LANES), index_map=lambda i: (i, 0))],
            out_specs=[pl.BlockSpec((SC_GATHER_WINDOW, d), index_map=lambda i: (i, 0))],
            core_axis_name=("core", "subcore"),
            dimension_semantics=(pltpu.PARALLEL,),
        )(idx_hbm, out_hbm)

    windows = idx.reshape(m // SC_GATHER_WINDOW, SC_GATHER_WINDOW)
    return gather(table, jnp.pad(windows, ((0, 0), (0, LANES - SC_GATHER_WINDOW))))


def _combine_kernel(rows_ref, x_ref, mod_ref, info_ref, fg_ref, o_ref, *, d_model, final):
    info = info_ref[...]
    rows = rows_ref[...]
    ff = info[:, R_W0:R_W0 + 1] * rows[:, :d_model] + info[:, R_W1:R_W1 + 1] * rows[:, d_model:]
    y = x_ref[...] + mod_ref[...][5:6] * ff
    o_ref[...] = _final_norm(y, fg_ref[...]) if final else y


def _combine(y_sorted, pos, x, mod, info, final_gain, final):
    b, s, d = x.shape
    tm = min(TM_COMBINE, s)
    nt = s // tm
    rows = _gather_rows(y_sorted, pos).reshape(b, s, TOP_K * d)
    return pl.pallas_call(
        functools.partial(_combine_kernel, d_model=d, final=final),
        grid=(b, nt),
        in_specs=[pl.BlockSpec((None, tm, TOP_K * d), lambda bi, i: (bi, i, 0)),
                  pl.BlockSpec((None, tm, d), lambda bi, i: (bi, i, 0)),
                  pl.BlockSpec((None, N_MOD, d), lambda bi, i: (bi, 0, 0)),
                  pl.BlockSpec((tm, LANES), lambda bi, i: (bi * nt + i, 0)),
                  _resident((1, d))],
        out_specs=pl.BlockSpec((None, tm, d), lambda bi, i: (bi, i, 0)),
        out_shape=jax.ShapeDtypeStruct(x.shape, F32),
        compiler_params=_params("parallel", "parallel"),
        name="moe_combine",
    )(rows, x, mod, info, final_gain.reshape(1, d))


def _moe_ffn(x, mod, router_w, router_b, w_gate, w_up, w_down, final_gain, final):
    b, s, d = x.shape
    n = b * s
    n_e = router_w.shape[-1]
    tm = TM_EXPERT
    hp, info, cnt = _router(x, mod, router_w, router_b)

    counts = cnt[0, :n_e].astype(jnp.int32)
    tiles = (counts + tm - 1) // tm
    tile_end = jnp.cumsum(tiles)
    row_start = (tile_end - tiles) * tm
    n_tiles = (TOP_K * n) // tm + n_e
    tile_expert = jnp.minimum(
        jnp.sum(jnp.arange(n_tiles, dtype=jnp.int32)[:, None] >= tile_end[None, :], axis=1),
        n_e - 1).astype(jnp.int32)
    n_used = tile_end[-1:].astype(jnp.int32)
    experts = info[:, R_E0:R_E1 + 1].astype(jnp.int32)
    ranks = info[:, R_RANK0:R_RANK1 + 1].astype(jnp.int32)
    pos = (row_start[experts] + ranks).reshape(-1)

    xs = _scatter_rows(hp, pos, n_tiles * tm)
    y_sorted = _grouped_experts(xs, tile_expert, n_used, w_gate, w_up, w_down)
    return _combine(y_sorted, pos, x, mod, info, final_gain, final)


def _rope_tables(seq_len):
    pos = jnp.arange(seq_len, dtype=jnp.int32)
    axis_pos = jnp.stack([pos // GRID_W, pos % GRID_W], axis=1).astype(F32)
    n_freq = HEAD_DIM // 4
    inv_freq = jnp.exp(-math.log(ROPE_THETA) * (2.0 * jnp.arange(n_freq, dtype=F32) / (HEAD_DIM // 2)))
    ang = axis_pos[:, :, None] * inv_freq[None, None, :]
    cos = jnp.cos(ang)
    sin = jnp.sin(ang)
    cos_t = jnp.concatenate([cos, cos], axis=-1).reshape(seq_len, HEAD_DIM)
    sin_t = jnp.concatenate([-sin, sin], axis=-1).reshape(seq_len, HEAD_DIM)
    return cos_t, sin_t


def kernel(x, c, w_mod, b_mod, w_in, q_norm_gain, k_norm_gain, conv_w, conv_b, lru_w_a, lru_b_a,
           lru_w_x, lru_b_x, lru_lambda, w_o_attn, w_o_lru, w_out, ffn_w_gate, ffn_w_up,
           ffn_w_down, router_w, router_b, moe_w_gate, moe_w_up, moe_w_down, final_gain):
    b, s, d = x.shape
    depth = w_in.shape[0]
    cos, sin = _rope_tables(s)
    mods = _modulation(c, w_mod, b_mod).reshape(depth, b, N_MOD, d)
    for l in range(depth):
        mod = mods[l]
        q, k, v, xr, gxg, sg = _inproj(x, mod, w_in[l].astype(BF16), q_norm_gain[l], k_norm_gain[l],
                                       cos, sin)
        attn0, attn1, h_fwd, h_bwd = _mixer(q, k, v, xr, conv_w[l], conv_b[l], lru_w_a[l], lru_b_a[l],
                                            lru_w_x[l], lru_b_x[l], lru_lambda[l], q_norm_gain[l])
        x = _merge(attn0, attn1, h_fwd, h_bwd, gxg, sg, x, mod, w_o_attn[l].astype(BF16),
                   w_o_lru[l].astype(BF16), w_out[l].astype(BF16))
        final = l == depth - 1
        j = l // 2
        if l % 2 == 0:
            x = _dense_ffn(x, mod, ffn_w_gate[j].astype(BF16), ffn_w_up[j].astype(BF16),
                           ffn_w_down[j].astype(BF16), final_gain, final)
        else:
            x = _moe_ffn(x, mod, router_w[j], router_b[j], moe_w_gate[j].astype(BF16),
                         moe_w_up[j].astype(BF16), moe_w_down[j].astype(BF16), final_gain, final)
    return x
```

```python
import functools
import math

import jax
import jax.numpy as jnp
from jax import lax
from jax.experimental import pallas as pl
from jax.experimental.pallas import tpu as pltpu
from jax.experimental.pallas import tpu_sc as plsc

HEAD_DIM = 128
N_Q_HEADS = 8
N_KV_HEADS = 2
Q_GROUP = N_Q_HEADS // N_KV_HEADS
GRID_W = 64
ROPE_THETA = 10000.0
ROPE_PAIR_SHIFT = HEAD_DIM // 4
LRU_BLOCKS = 8
LRU_C = 8.0
CONV_WIDTH = 4
N_MOD = 6
TOP_K = 2
NORM_EPS = 1e-6
LOG2_E = math.log2(math.e)
SOFTMAX_SAFE_LOG2_SHIFT = 55.0
BF16_ROUNDING_MARGIN = 1.0 + 2.0 ** -7

LANES = 128
SUBLANES = 8
VMEM_LIMIT_BYTES = 56 * 1024 * 1024

TM_INPROJ = 256
TQ_ATTN = 256
TM_MERGE = 512
TM_FFN = 512
TM_ROUTER = 256
TM_EXPERT = 256
TM_COMBINE = 256
FF_CHUNK = 512
KV_CHUNK = 512

BF16 = jnp.bfloat16
F32 = jnp.float32


def _params(*semantics):
    return pltpu.CompilerParams(dimension_semantics=semantics, vmem_limit_bytes=VMEM_LIMIT_BYTES)


def _resident(shape):
    zeros = (0,) * len(shape)
    return pl.BlockSpec(shape, lambda *_: zeros, pipeline_mode=pl.Buffered(1))


def _modulated_norm(x, shift, scale):
    ms = jnp.mean(x * x, axis=-1, keepdims=True)
    return x * lax.rsqrt(ms + NORM_EPS) * (1.0 + scale) + shift


def _mod_kernel(c_ref, w_ref, b_ref, o_ref):
    c = c_ref[...]
    act = c * jax.nn.sigmoid(c)
    o_ref[...] = jnp.dot(act, w_ref[...], preferred_element_type=F32,
                         precision=lax.Precision.HIGHEST) + b_ref[...]


def _modulation(c, w_mod, b_mod):
    depth, d, n = w_mod.shape
    b = c.shape[0]
    tn = 1536
    return pl.pallas_call(
        _mod_kernel,
        grid=(depth, n // tn),
        in_specs=[pl.BlockSpec((b, d), lambda l, j: (0, 0)),
                  pl.BlockSpec((None, d, tn), lambda l, j: (l, 0, j)),
                  pl.BlockSpec((None, 1, tn), lambda l, j: (l, 0, j))],
        out_specs=pl.BlockSpec((None, b, tn), lambda l, j: (l, 0, j)),
        out_shape=jax.ShapeDtypeStruct((depth, b, n), F32),
        compiler_params=_params("parallel", "parallel"),
        name="modulation",
    )(c, w_mod, b_mod.reshape(depth, 1, n))


def _inproj_kernel(x_ref, mod_ref, w_ref, qg_ref, kg_ref, cos_ref, sin_ref,
                   q_ref, k_ref, v_ref, xr_ref, gxg_ref, sg_ref, *, d_model):
    attn_w = N_Q_HEADS * HEAD_DIM
    kv_w = N_KV_HEADS * HEAD_DIM
    mod = mod_ref[...]
    hb = _modulated_norm(x_ref[...], mod[0:1], mod[1:2]).astype(BF16)

    cos = cos_ref[...]
    sin = sin_ref[...]
    lane = lax.broadcasted_iota(jnp.int32, cos.shape, 1)
    pair_first = (lane & ROPE_PAIR_SHIFT) == 0

    def norm_rope(y, gain, post_scale):
        r = lax.rsqrt(jnp.mean(y * y, axis=-1, keepdims=True) + NORM_EPS)
        yn = y * r * gain
        partner = jnp.where(pair_first,
                            pltpu.roll(yn, HEAD_DIM - ROPE_PAIR_SHIFT, 1),
                            pltpu.roll(yn, ROPE_PAIR_SHIFT, 1))
        return (yn * cos + partner * sin) * post_scale

    def proj(lo, width):
        return jnp.dot(hb, w_ref[:, lo:lo + width], preferred_element_type=F32)

    qf = proj(0, attn_w)
    qg = qg_ref[...]
    for h in range(N_Q_HEADS):
        sl = slice(h * HEAD_DIM, (h + 1) * HEAD_DIM)
        q_ref[:, sl] = norm_rope(qf[:, sl], qg, HEAD_DIM ** -0.5 * LOG2_E).astype(BF16)
    kf = proj(attn_w, kv_w)
    kg = kg_ref[...]
    for h in range(N_KV_HEADS):
        sl = slice(h * HEAD_DIM, (h + 1) * HEAD_DIM)
        k_ref[:, sl] = norm_rope(kf[:, sl], kg, 1.0).astype(BF16)
    v_ref[...] = proj(attn_w + kv_w, kv_w).astype(BF16)
    lo = attn_w + 2 * kv_w
    xr_ref[...] = proj(lo, d_model)
    gxg_ref[...] = jax.nn.gelu(proj(lo + d_model, d_model), approximate=True).astype(BF16)
    sg_ref[...] = jax.nn.sigmoid(proj(lo + 2 * d_model, 2 * d_model)).astype(BF16)


def _inproj(x, mod, w_in, q_gain, k_gain, cos, sin):
    b, s, d = x.shape
    tm = min(TM_INPROJ, s)
    attn_w = N_Q_HEADS * HEAD_DIM
    kv_w = N_KV_HEADS * HEAD_DIM
    row = lambda w: pl.BlockSpec((None, tm, w), lambda bi, i: (bi, i, 0))
    tab = pl.BlockSpec((tm, HEAD_DIM), lambda bi, i: (i, 0))
    outs = [(attn_w, BF16), (kv_w, BF16), (kv_w, BF16), (d, F32), (d, BF16), (2 * d, BF16)]
    return pl.pallas_call(
        functools.partial(_inproj_kernel, d_model=d),
        grid=(b, s // tm),
        in_specs=[row(d),
                  pl.BlockSpec((None, N_MOD, d), lambda bi, i: (bi, 0, 0)),
                  _resident(w_in.shape), _resident((1, HEAD_DIM)), _resident((1, HEAD_DIM)),
                  tab, tab],
        out_specs=[row(w) for w, _ in outs],
        out_shape=[jax.ShapeDtypeStruct((b, s, w), dt) for w, dt in outs],
        compiler_params=_params("parallel", "parallel"),
        name="inproj",
    )(x, mod, w_in, q_gain.reshape(1, HEAD_DIM), k_gain.reshape(1, HEAD_DIM), cos, sin)


def _lru_stage(xr_ref, prev_ref, next_ref, xs_scr, *, ts, tile, n_tiles):
    xs_scr[0:SUBLANES, :] = jnp.where(tile > 0, prev_ref[...], 0.0)
    xs_scr[SUBLANES:SUBLANES + ts, :] = xr_ref[...]
    xs_scr[SUBLANES + ts:, :] = jnp.where(tile < n_tiles - 1, next_ref[...], 0.0)


def _lru_block(n, cw_ref, cb_ref, wcat_ref, ba_ref, bx_ref, lam_ref, h_ref, xs_scr, carry_scr,
               *, ts, reverse):
    bw = LANES
    sl = slice(n * bw, (n + 1) * bw)

    def tap(shift):
        return xs_scr[SUBLANES + shift:SUBLANES + shift + ts, sl]

    cw = cw_ref[:, sl]
    xb = (cw[0:1] * tap(-2) + cw[1:2] * tap(-1) + cw[2:3] * tap(0) + cw[3:4] * tap(1)
          + cb_ref[:, sl])
    lam = lam_ref[:, sl]
    neg_c_softplus = -LRU_C * (jnp.maximum(-lam, 0.0) + jnp.log1p(jnp.exp(-jnp.abs(lam))))
    g = jnp.dot(xb.astype(BF16), wcat_ref[n], preferred_element_type=F32)
    r = jax.nn.sigmoid(g[:, :bw] + ba_ref[:, sl])
    i = jax.nn.sigmoid(g[:, bw:] + bx_ref[:, sl])
    log_a = neg_c_softplus * r
    a_all = jnp.exp(log_a)
    th = jnp.tanh(log_a)
    u_all = jnp.exp(0.5 * jnp.log(-2.0 * th / (1.0 - th))) * (i * xb)

    row = lax.broadcasted_iota(jnp.int32, (SUBLANES, bw), 0)
    n_groups = ts // SUBLANES
    state = carry_scr[0:1, sl]
    states = [None] * n_groups
    for v in (range(n_groups - 1, -1, -1) if reverse else range(n_groups)):
        rows = slice(v * SUBLANES, (v + 1) * SUBLANES)
        a = a_all[rows]
        u = u_all[rows]
        for d in (1, 2, 4):
            keep = (row < SUBLANES - d) if reverse else (row >= d)
            shift = (SUBLANES - d) if reverse else d
            a_prev = jnp.where(keep, pltpu.roll(a, shift, 0), 1.0)
            u_prev = jnp.where(keep, pltpu.roll(u, shift, 0), 0.0)
            u = a * u_prev + u
            a = a * a_prev
        h = u + a * state
        states[v] = h
        state = h[0:1] if reverse else h[SUBLANES - 1:SUBLANES]
    carry_scr[0:1, sl] = state
    h_ref[:, sl] = jnp.concatenate(states, axis=0).astype(BF16)


def _mixer_kernel(q_ref, k_ref, v_ref, xr_ref, prev_ref, next_ref, cw_ref, cb_ref, wcat_ref, ba_ref,
                  bx_ref, lam_ref, qg_ref, o_ref, h_ref, vext_scr, shift_scr, xs_scr, carry_scr,
                  *, tq, s_len, kv_chunk, n_tiles, reverse):
    nt = (((1,), (1,)), ((), ()))
    i = pl.program_id(1)

    @pl.when(i == 0)
    def _():
        carry_scr[...] = jnp.zeros_like(carry_scr)
        vext_scr[:, :HEAD_DIM] = v_ref[...]
        vext_scr[:, HEAD_DIM:] = jnp.ones((s_len, HEAD_DIM), BF16)
        kf = k_ref[...].astype(F32)
        k_norm_max = jnp.max(jnp.sqrt(jnp.sum(kf * kf, axis=-1, keepdims=True)))
        q_norm_max = jnp.max(jnp.abs(qg_ref[...])) * (LOG2_E * BF16_ROUNDING_MARGIN)
        shift_scr[0] = q_norm_max * k_norm_max

    lru_stage = functools.partial(_lru_stage, xr_ref, prev_ref, next_ref, xs_scr, ts=tq,
                                  tile=(n_tiles - 1 - i) if reverse else i, n_tiles=n_tiles)
    lru_block = functools.partial(_lru_block, cw_ref=cw_ref, cb_ref=cb_ref, wcat_ref=wcat_ref,
                                  ba_ref=ba_ref, bx_ref=bx_ref, lam_ref=lam_ref, h_ref=h_ref,
                                  xs_scr=xs_scr, carry_scr=carry_scr, ts=tq, reverse=reverse)

    q = q_ref[...]
    qs = jnp.concatenate([q[:, h * HEAD_DIM:(h + 1) * HEAD_DIM] for h in range(Q_GROUP)], axis=0)
    shift = shift_scr[0]
    safe = shift <= SOFTMAX_SAFE_LOG2_SHIFT

    def finish(acc):
        o = acc[:, :HEAD_DIM] / acc[:, HEAD_DIM:]
        for h in range(Q_GROUP):
            o_ref[:, h * HEAD_DIM:(h + 1) * HEAD_DIM] = o[h * tq:(h + 1) * tq].astype(BF16)

    @pl.when(safe)
    def _():
        lru_stage()
        chunks = list(range(0, s_len, kv_chunk))
        per_chunk = -(-LRU_BLOCKS // len(chunks))
        acc = None
        for c, lo in enumerate(chunks):
            s = lax.dot_general(qs, k_ref[lo:lo + kv_chunk, :], nt, preferred_element_type=F32)
            p = jnp.exp2(s - shift).astype(BF16)
            pv = jnp.dot(p, vext_scr[lo:lo + kv_chunk, :], preferred_element_type=F32)
            acc = pv if acc is None else acc + pv
            for n in range(c * per_chunk, min((c + 1) * per_chunk, LRU_BLOCKS)):
                lru_block(n)
        finish(acc)

    @pl.when(jnp.logical_not(safe))
    def _():
        s = lax.dot_general(qs, k_ref[...], nt, preferred_element_type=F32)
        p = jnp.exp2(s - jnp.max(s, axis=-1, keepdims=True)).astype(BF16)
        finish(jnp.dot(p, vext_scr[...], preferred_element_type=F32))
        lru_stage()
        for n in range(LRU_BLOCKS):
            lru_block(n)


def _mixer_group(q, k, v, xr, conv_w, conv_b, wcat, b_a, b_x, lam, q_gain, group, reverse):
    b, s, w = xr.shape
    tq = min(TQ_ATTN, s)
    n_tiles = s // tq
    blocks8 = tq // SUBLANES
    gw = Q_GROUP * HEAD_DIM
    tile = (lambda i: n_tiles - 1 - i) if reverse else (lambda i: i)
    qspec = pl.BlockSpec((None, tq, gw), lambda bi, i: (bi, i, group))
    kvspec = pl.BlockSpec((None, s, HEAD_DIM), lambda bi, i: (bi, 0, group))
    main = pl.BlockSpec((None, tq, w), lambda bi, i: (bi, tile(i), 0))
    prev = pl.BlockSpec((None, SUBLANES, w),
                        lambda bi, i: (bi, jnp.maximum(tile(i) * blocks8 - 1, 0), 0))
    nxt = pl.BlockSpec((None, SUBLANES, w),
                       lambda bi, i: (bi, jnp.minimum((tile(i) + 1) * blocks8, s // SUBLANES - 1), 0))
    vec = _resident((1, w))
    return pl.pallas_call(
        functools.partial(_mixer_kernel, tq=tq, s_len=s, kv_chunk=min(KV_CHUNK, s), n_tiles=n_tiles,
                          reverse=reverse),
        grid=(b, n_tiles),
        in_specs=[qspec, kvspec, kvspec, main, prev, nxt, _resident((CONV_WIDTH, w)), vec,
                  _resident(wcat.shape), vec, vec, vec, _resident((1, HEAD_DIM))],
        out_specs=[pl.BlockSpec((None, tq, gw), lambda bi, i: (bi, i, 0)), main],
        out_shape=[jax.ShapeDtypeStruct((b, s, gw), BF16), jax.ShapeDtypeStruct((b, s, w), BF16)],
        scratch_shapes=[pltpu.VMEM((s, 2 * HEAD_DIM), BF16), pltpu.SMEM((1,), F32),
                        pltpu.VMEM((tq + 2 * SUBLANES, w), F32), pltpu.VMEM((SUBLANES, w), F32)],
        compiler_params=_params("parallel", "arbitrary"),
        name="mixer_bwd" if reverse else "mixer_fwd",
    )(q, k, v, xr, xr, xr, conv_w, conv_b.reshape(1, w), wcat, b_a.reshape(1, w), b_x.reshape(1, w),
      lam.reshape(1, w), q_gain.reshape(1, HEAD_DIM))


def _mixer(q, k, v, xr, conv_w, conv_b, w_a, b_a, w_x, b_x, lam, q_gain):
    wcat = jnp.concatenate([w_a, w_x], axis=-1).astype(BF16)
    outs = [_mixer_group(q, k, v, xr, conv_w, conv_b, wcat[g], b_a[g], b_x[g], lam[g], q_gain, g,
                         g == 1)
            for g in range(N_KV_HEADS)]
    (attn0, h_fwd), (attn1, h_bwd) = outs
    return attn0, attn1, h_fwd, h_bwd


def _merge_kernel(attn0_ref, attn1_ref, hf_ref, hb_ref, gxg_ref, sg_ref, x_ref, mod_ref, woa_ref,
                  wol_ref, wout_ref, o_ref, *, d_model):
    gw = attn0_ref.shape[-1]
    ya = (jnp.dot(attn0_ref[...], woa_ref[:gw, :], preferred_element_type=F32)
          + jnp.dot(attn1_ref[...], woa_ref[gw:, :], preferred_element_type=F32))
    ylru = (hf_ref[...].astype(F32) + hb_ref[...].astype(F32)) * gxg_ref[...].astype(F32)
    yl = jnp.dot(ylru.astype(BF16), wol_ref[...], preferred_element_type=F32)
    sg = sg_ref[...].astype(F32)
    merged = sg[:, :d_model] * ya + sg[:, d_model:] * yl
    out = jnp.dot(merged.astype(BF16), wout_ref[...], preferred_element_type=F32)
    o_ref[...] = x_ref[...] + mod_ref[...][2:3] * out


def _merge(attn0, attn1, h_fwd, h_bwd, gxg, sg, x, mod, w_o_attn, w_o_lru, w_out):
    b, s, d = x.shape
    tm = min(TM_MERGE, s)
    row = lambda w: pl.BlockSpec((None, tm, w), lambda bi, i: (bi, i, 0))
    gw = attn0.shape[-1]
    return pl.pallas_call(
        functools.partial(_merge_kernel, d_model=d),
        grid=(b, s // tm),
        in_specs=[row(gw), row(gw), row(d), row(d), row(d), row(2 * d), row(d),
                  pl.BlockSpec((None, N_MOD, d), lambda bi, i: (bi, 0, 0)),
                  _resident(w_o_attn.shape), _resident(w_o_lru.shape), _resident(w_out.shape)],
        out_specs=row(d),
        out_shape=jax.ShapeDtypeStruct(x.shape, F32),
        compiler_params=_params("parallel", "parallel"),
        name="merge",
    )(attn0, attn1, h_fwd, h_bwd, gxg, sg, x, mod, w_o_attn, w_o_lru, w_out)


def _swiglu_chunks(hb, wg_ref, wu_ref, wd_ref):
    d_ff = wg_ref.shape[-1]
    acc = None
    for lo in range(0, d_ff, FF_CHUNK):
        hi = min(lo + FF_CHUNK, d_ff)
        g = jnp.dot(hb, wg_ref[:, lo:hi], preferred_element_type=F32)
        u = jnp.dot(hb, wu_ref[:, lo:hi], preferred_element_type=F32)
        act = (g * jax.nn.sigmoid(g) * u).astype(BF16)
        y = jnp.dot(act, wd_ref[lo:hi, :], preferred_element_type=F32)
        acc = y if acc is None else acc + y
    return acc


def _final_norm(x, gain):
    return x * lax.rsqrt(jnp.mean(x * x, axis=-1, keepdims=True) + NORM_EPS) * gain


def _ffn_kernel(x_ref, mod_ref, wg_ref, wu_ref, wd_ref, fg_ref, o_ref, *, final):
    x = x_ref[...]
    mod = mod_ref[...]
    hb = _modulated_norm(x, mod[3:4], mod[4:5]).astype(BF16)
    y = x + mod[5:6] * _swiglu_chunks(hb, wg_ref, wu_ref, wd_ref)
    o_ref[...] = _final_norm(y, fg_ref[...]) if final else y


def _dense_ffn(x, mod, w_gate, w_up, w_down, final_gain, final):
    b, s, d = x.shape
    tm = min(TM_FFN, s)
    row = pl.BlockSpec((None, tm, d), lambda bi, i: (bi, i, 0))
    return pl.pallas_call(
        functools.partial(_ffn_kernel, final=final),
        grid=(b, s // tm),
        in_specs=[row, pl.BlockSpec((None, N_MOD, d), lambda bi, i: (bi, 0, 0)),
                  _resident(w_gate.shape), _resident(w_up.shape), _resident(w_down.shape),
                  _resident((1, d))],
        out_specs=row,
        out_shape=jax.ShapeDtypeStruct(x.shape, F32),
        compiler_params=_params("parallel", "parallel"),
        name="dense_ffn",
    )(x, mod, w_gate, w_up, w_down, final_gain.reshape(1, d))


R_E0, R_E1, R_RANK0, R_RANK1, R_W0, R_W1 = range(6)


def _router_kernel(x_ref, mod_ref, rw_ref, rb_ref, hp_ref, info_ref, cnt_ref, carry_scr, *, tm):
    @pl.when((pl.program_id(0) == 0) & (pl.program_id(1) == 0))
    def _():
        carry_scr[...] = jnp.zeros_like(carry_scr)

    mod = mod_ref[...]
    h = _modulated_norm(x_ref[...], mod[3:4], mod[4:5])
    hp_ref[...] = h

    logits = jnp.dot(h, rw_ref[...], preferred_element_type=F32,
                     precision=lax.Precision.HIGHEST) + rb_ref[...]
    lane = lax.broadcasted_iota(jnp.int32, logits.shape, 1)
    m1 = jnp.max(logits, axis=-1, keepdims=True)
    e0 = jnp.min(jnp.where(logits == m1, lane, LANES), axis=-1, keepdims=True)
    rest = jnp.where(lane == e0, -jnp.inf, logits)
    m2 = jnp.max(rest, axis=-1, keepdims=True)
    e1 = jnp.min(jnp.where(rest == m2, lane, LANES), axis=-1, keepdims=True)
    t = jnp.exp(m2 - m1)
    w0 = 1.0 / (1.0 + t)
    w1 = t * w0

    hit0 = lane == e0
    hit1 = lane == e1
    onehot = jnp.where(hit0 | hit1, 1.0, 0.0)
    ri = lax.broadcasted_iota(jnp.int32, (tm, tm), 0)
    ci = lax.broadcasted_iota(jnp.int32, (tm, tm), 1)
    before = jnp.where(ci < ri, 1.0, 0.0).astype(BF16)
    seen = jnp.dot(before, onehot.astype(BF16), preferred_element_type=F32) + carry_scr[0:1, :]
    rank0 = jnp.sum(jnp.where(hit0, seen, 0.0), axis=-1, keepdims=True)
    rank1 = jnp.sum(jnp.where(hit1, seen, 0.0), axis=-1, keepdims=True)
    carry = carry_scr[0:1, :] + jnp.sum(onehot, axis=0, keepdims=True)
    carry_scr[0:1, :] = carry
    cnt_ref[...] = jnp.broadcast_to(carry, cnt_ref.shape)

    info = jnp.zeros(logits.shape, F32)
    for col, val in ((R_E0, e0.astype(F32)), (R_E1, e1.astype(F32)), (R_RANK0, rank0),
                     (R_RANK1, rank1), (R_W0, w0), (R_W1, w1)):
        info = jnp.where(lane == col, val, info)
    info_ref[...] = info


def _router(x, mod, router_w, router_b):
    b, s, d = x.shape
    tm = min(TM_ROUTER, s)
    n_e = router_w.shape[-1]
    rw = jnp.zeros((d, LANES), F32).at[:, :n_e].set(router_w)
    rb = jnp.full((1, LANES), -jnp.inf, F32).at[0, :n_e].set(router_b)
    nt = s // tm
    return pl.pallas_call(
        functools.partial(_router_kernel, tm=tm),
        grid=(b, nt),
        in_specs=[pl.BlockSpec((None, tm, d), lambda bi, i: (bi, i, 0)),
                  pl.BlockSpec((None, N_MOD, d), lambda bi, i: (bi, 0, 0)),
                  _resident(rw.shape), _resident(rb.shape)],
        out_specs=[pl.BlockSpec((tm, d), lambda bi, i: (bi * nt + i, 0)),
                   pl.BlockSpec((tm, LANES), lambda bi, i: (bi * nt + i, 0)),
                   _resident((SUBLANES, LANES))],
        out_shape=[jax.ShapeDtypeStruct((b * s, d), F32),
                   jax.ShapeDtypeStruct((b * s, LANES), F32),
                   jax.ShapeDtypeStruct((SUBLANES, LANES), F32)],
        scratch_shapes=[pltpu.VMEM((SUBLANES, LANES), F32)],
        compiler_params=_params("arbitrary", "arbitrary"),
        name="router",
    )(x, mod, rw, rb)


SC_GATHER_WINDOW = 32


def _gather_rows(table, idx):
    m = idx.shape[0]
    d = table.shape[1]
    mesh = plsc.VectorSubcoreMesh(core_axis_name="core", subcore_axis_name="subcore")

    @functools.partial(pl.kernel, out_type=jax.ShapeDtypeStruct((m, d), table.dtype), mesh=mesh,
                       scratch_types=[])
    def gather(table_hbm, idx_hbm, out_hbm):
        def body(idx_vmem, out_vmem):
            pltpu.sync_copy(table_hbm.at[idx_vmem.at[0, pl.ds(0, SC_GATHER_WINDOW)]], out_vmem)

        pltpu.emit_pipeline(
            body,
            grid=(m // SC_GATHER_WINDOW,),
            in_specs=[pl.BlockSpec((1, LANES), index_map=lambda i: (i, 0))],
            out_specs=[pl.BlockSpec((SC_GATHER_WINDOW, d), index_map=lambda i: (i, 0))],
            core_axis_name=("core", "subcore"),
            dimension_semantics=(pltpu.PARALLEL,),
        )(idx_hbm, out_hbm)

    windows = idx.reshape(m // SC_GATHER_WINDOW, SC_GATHER_WINDOW)
    return gather(table, jnp.pad(windows, ((0, 0), (0, LANES - SC_GATHER_WINDOW))))


def _expert_kernel(te_ref, nu_ref, xs_ref, wg_ref, wu_ref, wd_ref, y_ref):
    del te_ref
    i = pl.program_id(0)

    @pl.when(i < nu_ref[0])
    def _():
        y_ref[...] = _swiglu_chunks(xs_ref[...].astype(BF16), wg_ref, wu_ref, wd_ref)

    @pl.when(i >= nu_ref[0])
    def _():
        y_ref[...] = jnp.zeros_like(y_ref)


def _grouped_experts(xs, tile_expert, n_used, w_gate, w_up, w_down):
    n_rows, d = xs.shape
    tm = TM_EXPERT
    ff = w_gate.shape[-1]
    grid_spec = pltpu.PrefetchScalarGridSpec(
        num_scalar_prefetch=2,
        grid=(n_rows // tm,),
        in_specs=[pl.BlockSpec((tm, d), lambda i, te, nu: (i, 0)),
                  pl.BlockSpec((None, d, ff), lambda i, te, nu: (te[i], 0, 0)),
                  pl.BlockSpec((None, d, ff), lambda i, te, nu: (te[i], 0, 0)),
                  pl.BlockSpec((None, ff, d), lambda i, te, nu: (te[i], 0, 0))],
        out_specs=pl.BlockSpec((tm, d), lambda i, te, nu: (i, 0)),
    )
    return pl.pallas_call(
        _expert_kernel,
        grid_spec=grid_spec,
        out_shape=jax.ShapeDtypeStruct((n_rows, d), F32),
        compiler_params=pltpu.CompilerParams(dimension_semantics=("arbitrary",),
                                             vmem_limit_bytes=60 * 1024 * 1024),
        name="moe_experts",
    )(tile_expert, n_used, xs, w_gate, w_up, w_down)


def _combine_kernel(rows0_ref, rows1_ref, x_ref, mod_ref, info_ref, fg_ref, o_ref, *, final):
    info = info_ref[...]
    ff = info[:, R_W0:R_W0 + 1] * rows0_ref[...] + info[:, R_W1:R_W1 + 1] * rows1_ref[...]
    y = x_ref[...] + mod_ref[...][5:6] * ff
    o_ref[...] = _final_norm(y, fg_ref[...]) if final else y


def _combine(y_sorted, pos, x, mod, info, final_gain, final):
    b, s, d = x.shape
    tm = min(TM_COMBINE, s)
    nt = s // tm
    rows = _gather_rows(y_sorted, pos.reshape(b * s, TOP_K).T.reshape(-1))
    second = (b * s) // tm
    return pl.pallas_call(
        functools.partial(_combine_kernel, final=final),
        grid=(b, nt),
        in_specs=[pl.BlockSpec((tm, d), lambda bi, i: (bi * nt + i, 0)),
                  pl.BlockSpec((tm, d), lambda bi, i: (second + bi * nt + i, 0)),
                  pl.BlockSpec((None, tm, d), lambda bi, i: (bi, i, 0)),
                  pl.BlockSpec((None, N_MOD, d), lambda bi, i: (bi, 0, 0)),
                  pl.BlockSpec((tm, LANES), lambda bi, i: (bi * nt + i, 0)),
                  _resident((1, d))],
        out_specs=pl.BlockSpec((None, tm, d), lambda bi, i: (bi, i, 0)),
        out_shape=jax.ShapeDtypeStruct(x.shape, F32),
        compiler_params=_params("parallel", "parallel"),
        name="moe_combine",
    )(rows, rows, x, mod, info, final_gain.reshape(1, d))


def _moe_ffn(x, mod, router_w, router_b, w_gate, w_up, w_down, final_gain, final):
    b, s, d = x.shape
    n = b * s
    n_e = router_w.shape[-1]
    tm = TM_EXPERT
    hp, info, cnt = _router(x, mod, router_w, router_b)

    counts = cnt[0, :n_e].astype(jnp.int32)
    tiles = (counts + tm - 1) // tm
    tile_end = jnp.cumsum(tiles)
    row_start = (tile_end - tiles) * tm
    n_tiles = (TOP_K * n) // tm + n_e
    tile_expert = jnp.minimum(
        jnp.sum(jnp.arange(n_tiles, dtype=jnp.int32)[:, None] >= tile_end[None, :], axis=1),
        n_e - 1).astype(jnp.int32)
    n_used = tile_end[-1:].astype(jnp.int32)
    experts = info[:, R_E0:R_E1 + 1].astype(jnp.int32)
    ranks = info[:, R_RANK0:R_RANK1 + 1].astype(jnp.int32)
    pos = (row_start[experts] + ranks).reshape(-1)
    src = jnp.zeros((n_tiles * tm,), jnp.int32).at[pos].set(
        jnp.arange(TOP_K * n, dtype=jnp.int32) // TOP_K)

    xs = _gather_rows(hp, src)
    y_sorted = _grouped_experts(xs, tile_expert, n_used, w_gate, w_up, w_down)
    return _combine(y_sorted, pos, x, mod, info, final_gain, final)


def _rope_tables(seq_len):
    pos = jnp.arange(seq_len, dtype=jnp.int32)
    axis_pos = jnp.stack([pos // GRID_W, pos % GRID_W], axis=1).astype(F32)
    n_freq = HEAD_DIM // 4
    inv_freq = jnp.exp(-math.log(ROPE_THETA) * (2.0 * jnp.arange(n_freq, dtype=F32) / (HEAD_DIM // 2)))
    ang = axis_pos[:, :, None] * inv_freq[None, None, :]
    cos = jnp.cos(ang)
    sin = jnp.sin(ang)
    cos_t = jnp.concatenate([cos, cos], axis=-1).reshape(seq_len, HEAD_DIM)
    sin_t = jnp.concatenate([-sin, sin], axis=-1).reshape(seq_len, HEAD_DIM)
    return cos_t, sin_t


def kernel(x, c, w_mod, b_mod, w_in, q_norm_gain, k_norm_gain, conv_w, conv_b, lru_w_a, lru_b_a,
           lru_w_x, lru_b_x, lru_lambda, w_o_attn, w_o_lru, w_out, ffn_w_gate, ffn_w_up,
           ffn_w_down, router_w, router_b, moe_w_gate, moe_w_up, moe_w_down, final_gain):
    b, s, d = x.shape
    depth = w_in.shape[0]
    cos, sin = _rope_tables(s)
    mods = _modulation(c, w_mod, b_mod).reshape(depth, b, N_MOD, d)
    for l in range(depth):
        mod = mods[l]
        q, k, v, xr, gxg, sg = _inproj(x, mod, w_in[l].astype(BF16), q_norm_gain[l], k_norm_gain[l],
                                       cos, sin)
        attn0, attn1, h_fwd, h_bwd = _mixer(q, k, v, xr, conv_w[l], conv_b[l], lru_w_a[l], lru_b_a[l],
                                            lru_w_x[l], lru_b_x[l], lru_lambda[l], q_norm_gain[l])
        x = _merge(attn0, attn1, h_fwd, h_bwd, gxg, sg, x, mod, w_o_attn[l].astype(BF16),
                   w_o_lru[l].astype(BF16), w_out[l].astype(BF16))
        final = l == depth - 1
        j = l // 2
        if l % 2 == 0:
            x = _dense_ffn(x, mod, ffn_w_gate[j].astype(BF16), ffn_w_up[j].astype(BF16),
                           ffn_w_down[j].astype(BF16), final_gain, final)
        else:
            x = _moe_ffn(x, mod, router_w[j], router_b[j], moe_w_gate[j].astype(BF16),
                         moe_w_up[j].astype(BF16), moe_w_down[j].astype(BF16), final_gain, final)
    return x
```

```python
import functools
import math

import jax
import jax.numpy as jnp
from jax import lax
from jax.experimental import pallas as pl
from jax.experimental.pallas import tpu as pltpu
from jax.experimental.pallas import tpu_sc as plsc

HEAD_DIM = 128
N_Q_HEADS = 8
N_KV_HEADS = 2
Q_GROUP = N_Q_HEADS // N_KV_HEADS
GRID_W = 64
ROPE_THETA = 10000.0
ROPE_PAIR_SHIFT = HEAD_DIM // 4
LRU_BLOCKS = 8
LRU_C = 8.0
CONV_WIDTH = 4
N_MOD = 6
TOP_K = 2
NORM_EPS = 1e-6
LOG2_E = math.log2(math.e)
SOFTMAX_SAFE_LOG2_SHIFT = 55.0
BF16_ROUNDING_MARGIN = 1.0 + 2.0 ** -7

LANES = 128
SUBLANES = 8
VMEM_LIMIT_BYTES = 56 * 1024 * 1024

TM_INPROJ = 256
TQ_ATTN = 256
TM_MERGE = 512
TM_FFN = 512
TM_ROUTER = 256
TM_EXPERT = 256
TM_COMBINE = 256
FF_CHUNK = 512
KV_CHUNK = 512

BF16 = jnp.bfloat16
F32 = jnp.float32


def _params(*semantics):
    return pltpu.CompilerParams(dimension_semantics=semantics, vmem_limit_bytes=VMEM_LIMIT_BYTES)


def _resident(shape):
    zeros = (0,) * len(shape)
    return pl.BlockSpec(shape, lambda *_: zeros, pipeline_mode=pl.Buffered(1))


def _modulated_norm(x, shift, scale):
    ms = jnp.mean(x * x, axis=-1, keepdims=True)
    return x * lax.rsqrt(ms + NORM_EPS) * (1.0 + scale) + shift


def _mod_kernel(c_ref, w_ref, b_ref, o_ref):
    c = c_ref[...]
    act = c * jax.nn.sigmoid(c)
    o_ref[...] = jnp.dot(act, w_ref[...], preferred_element_type=F32,
                         precision=lax.Precision.HIGHEST) + b_ref[...]


def _modulation(c, w_mod, b_mod):
    depth, d, n = w_mod.shape
    b = c.shape[0]
    tn = 1536
    return pl.pallas_call(
        _mod_kernel,
        grid=(depth, n // tn),
        in_specs=[pl.BlockSpec((b, d), lambda l, j: (0, 0)),
                  pl.BlockSpec((None, d, tn), lambda l, j: (l, 0, j)),
                  pl.BlockSpec((None, 1, tn), lambda l, j: (l, 0, j))],
        out_specs=pl.BlockSpec((None, b, tn), lambda l, j: (l, 0, j)),
        out_shape=jax.ShapeDtypeStruct((depth, b, n), F32),
        compiler_params=_params("parallel", "parallel"),
        name="modulation",
    )(c, w_mod, b_mod.reshape(depth, 1, n))


def _inproj_kernel(x_ref, mod_ref, w_ref, qg_ref, kg_ref, cos_ref, sin_ref,
                   q_ref, k_ref, v_ref, xr_ref, gxg_ref, sg_ref, *, d_model):
    attn_w = N_Q_HEADS * HEAD_DIM
    kv_w = N_KV_HEADS * HEAD_DIM
    mod = mod_ref[...]
    hb = _modulated_norm(x_ref[...], mod[0:1], mod[1:2]).astype(BF16)

    cos = cos_ref[...]
    sin = sin_ref[...]
    lane = lax.broadcasted_iota(jnp.int32, cos.shape, 1)
    pair_first = (lane & ROPE_PAIR_SHIFT) == 0

    def norm_rope(y, gain, post_scale):
        r = lax.rsqrt(jnp.mean(y * y, axis=-1, keepdims=True) + NORM_EPS)
        yn = y * r * gain
        partner = jnp.where(pair_first,
                            pltpu.roll(yn, HEAD_DIM - ROPE_PAIR_SHIFT, 1),
                            pltpu.roll(yn, ROPE_PAIR_SHIFT, 1))
        return (yn * cos + partner * sin) * post_scale

    def proj(lo, width):
        return jnp.dot(hb, w_ref[:, lo:lo + width], preferred_element_type=F32)

    qf = proj(0, attn_w)
    qg = qg_ref[...]
    for h in range(N_Q_HEADS):
        sl = slice(h * HEAD_DIM, (h + 1) * HEAD_DIM)
        q_ref[:, sl] = norm_rope(qf[:, sl], qg, HEAD_DIM ** -0.5 * LOG2_E).astype(BF16)
    kf = proj(attn_w, kv_w)
    kg = kg_ref[...]
    for h in range(N_KV_HEADS):
        sl = slice(h * HEAD_DIM, (h + 1) * HEAD_DIM)
        k_ref[:, sl] = norm_rope(kf[:, sl], kg, 1.0).astype(BF16)
    v_ref[...] = proj(attn_w + kv_w, kv_w).astype(BF16)
    lo = attn_w + 2 * kv_w
    xr_ref[...] = proj(lo, d_model)
    gxg_ref[...] = jax.nn.gelu(proj(lo + d_model, d_model), approximate=True).astype(BF16)
    sg_ref[...] = jax.nn.sigmoid(proj(lo + 2 * d_model, 2 * d_model)).astype(BF16)


def _inproj(x, mod, w_in, q_gain, k_gain, cos, sin):
    b, s, d = x.shape
    tm = min(TM_INPROJ, s)
    attn_w = N_Q_HEADS * HEAD_DIM
    kv_w = N_KV_HEADS * HEAD_DIM
    row = lambda w: pl.BlockSpec((None, tm, w), lambda bi, i: (bi, i, 0))
    tab = pl.BlockSpec((tm, HEAD_DIM), lambda bi, i: (i, 0))
    outs = [(attn_w, BF16), (kv_w, BF16), (kv_w, BF16), (d, F32), (d, BF16), (2 * d, BF16)]
    return pl.pallas_call(
        functools.partial(_inproj_kernel, d_model=d),
        grid=(b, s // tm),
        in_specs=[row(d),
                  pl.BlockSpec((None, N_MOD, d), lambda bi, i: (bi, 0, 0)),
                  _resident(w_in.shape), _resident((1, HEAD_DIM)), _resident((1, HEAD_DIM)),
                  tab, tab],
        out_specs=[row(w) for w, _ in outs],
        out_shape=[jax.ShapeDtypeStruct((b, s, w), dt) for w, dt in outs],
        compiler_params=_params("parallel", "parallel"),
        name="inproj",
    )(x, mod, w_in, q_gain.reshape(1, HEAD_DIM), k_gain.reshape(1, HEAD_DIM), cos, sin)


def _lru_stage(xr_ref, prev_ref, next_ref, xs_scr, *, ts, tile, n_tiles):
    xs_scr[0:SUBLANES, :] = jnp.where(tile > 0, prev_ref[...], 0.0)
    xs_scr[SUBLANES:SUBLANES + ts, :] = xr_ref[...]
    xs_scr[SUBLANES + ts:, :] = jnp.where(tile < n_tiles - 1, next_ref[...], 0.0)


def _lru_block(n, cw_ref, cb_ref, wcat_ref, ba_ref, bx_ref, lam_ref, h_ref, xs_scr, carry_scr,
               *, ts, reverse):
    bw = LANES
    sl = slice(n * bw, (n + 1) * bw)

    def tap(shift):
        return xs_scr[SUBLANES + shift:SUBLANES + shift + ts, sl]

    cw = cw_ref[:, sl]
    xb = (cw[0:1] * tap(-2) + cw[1:2] * tap(-1) + cw[2:3] * tap(0) + cw[3:4] * tap(1)
          + cb_ref[:, sl])
    lam = lam_ref[:, sl]
    neg_c_softplus = -LRU_C * (jnp.maximum(-lam, 0.0) + jnp.log1p(jnp.exp(-jnp.abs(lam))))
    g = jnp.dot(xb.astype(BF16), wcat_ref[n], preferred_element_type=F32)
    r = jax.nn.sigmoid(g[:, :bw] + ba_ref[:, sl])
    i = jax.nn.sigmoid(g[:, bw:] + bx_ref[:, sl])
    log_a = neg_c_softplus * r
    a_all = jnp.exp(log_a)
    th = jnp.tanh(log_a)
    u_all = jnp.exp(0.5 * jnp.log(-2.0 * th / (1.0 - th))) * (i * xb)

    row = lax.broadcasted_iota(jnp.int32, (SUBLANES, bw), 0)
    n_groups = ts // SUBLANES
    state = carry_scr[0:1, sl]
    states = [None] * n_groups
    for v in (range(n_groups - 1, -1, -1) if reverse else range(n_groups)):
        rows = slice(v * SUBLANES, (v + 1) * SUBLANES)
        a = a_all[rows]
        u = u_all[rows]
        for d in (1, 2, 4):
            keep = (row < SUBLANES - d) if reverse else (row >= d)
            shift = (SUBLANES - d) if reverse else d
            a_prev = jnp.where(keep, pltpu.roll(a, shift, 0), 1.0)
            u_prev = jnp.where(keep, pltpu.roll(u, shift, 0), 0.0)
            u = a * u_prev + u
            a = a * a_prev
        h = u + a * state
        states[v] = h
        state = h[0:1] if reverse else h[SUBLANES - 1:SUBLANES]
    carry_scr[0:1, sl] = state
    h_ref[:, sl] = jnp.concatenate(states, axis=0).astype(BF16)


def _mixer_kernel(q_ref, k_ref, v_ref, xr_ref, prev_ref, next_ref, cw_ref, cb_ref, wcat_ref, ba_ref,
                  bx_ref, lam_ref, qg_ref, o_ref, h_ref, vext_scr, shift_scr, xs_scr, carry_scr,
                  *, tq, s_len, kv_chunk, n_tiles, reverse):
    nt = (((1,), (1,)), ((), ()))
    i = pl.program_id(1)

    @pl.when(i == 0)
    def _():
        carry_scr[...] = jnp.zeros_like(carry_scr)
        vext_scr[:, :HEAD_DIM] = v_ref[...]
        vext_scr[:, HEAD_DIM:] = jnp.ones((s_len, HEAD_DIM), BF16)
        kf = k_ref[...].astype(F32)
        k_norm_max = jnp.max(jnp.sqrt(jnp.sum(kf * kf, axis=-1, keepdims=True)))
        q_norm_max = jnp.max(jnp.abs(qg_ref[...])) * (LOG2_E * BF16_ROUNDING_MARGIN)
        shift_scr[0] = q_norm_max * k_norm_max

    lru_stage = functools.partial(_lru_stage, xr_ref, prev_ref, next_ref, xs_scr, ts=tq,
                                  tile=(n_tiles - 1 - i) if reverse else i, n_tiles=n_tiles)
    lru_block = functools.partial(_lru_block, cw_ref=cw_ref, cb_ref=cb_ref, wcat_ref=wcat_ref,
                                  ba_ref=ba_ref, bx_ref=bx_ref, lam_ref=lam_ref, h_ref=h_ref,
                                  xs_scr=xs_scr, carry_scr=carry_scr, ts=tq, reverse=reverse)

    q = q_ref[...]
    qs = jnp.concatenate([q[:, h * HEAD_DIM:(h + 1) * HEAD_DIM] for h in range(Q_GROUP)], axis=0)
    shift = shift_scr[0]
    safe = shift <= SOFTMAX_SAFE_LOG2_SHIFT

    def finish(acc):
        o = acc[:, :HEAD_DIM] / acc[:, HEAD_DIM:]
        for h in range(Q_GROUP):
            o_ref[:, h * HEAD_DIM:(h + 1) * HEAD_DIM] = o[h * tq:(h + 1) * tq].astype(BF16)

    @pl.when(safe)
    def _():
        lru_stage()
        chunks = list(range(0, s_len, kv_chunk))
        per_chunk = -(-LRU_BLOCKS // len(chunks))
        acc = None
        for c, lo in enumerate(chunks):
            s = lax.dot_general(qs, k_ref[lo:lo + kv_chunk, :], nt, preferred_element_type=F32)
            p = jnp.exp2(s - shift).astype(BF16)
            pv = jnp.dot(p, vext_scr[lo:lo + kv_chunk, :], preferred_element_type=F32)
            acc = pv if acc is None else acc + pv
            for n in range(c * per_chunk, min((c + 1) * per_chunk, LRU_BLOCKS)):
                lru_block(n)
        finish(acc)

    @pl.when(jnp.logical_not(safe))
    def _():
        s = lax.dot_general(qs, k_ref[...], nt, preferred_element_type=F32)
        p = jnp.exp2(s - jnp.max(s, axis=-1, keepdims=True)).astype(BF16)
        finish(jnp.dot(p, vext_scr[...], preferred_element_type=F32))
        lru_stage()
        for n in range(LRU_BLOCKS):
            lru_block(n)


def _mixer_group(q, k, v, xr, conv_w, conv_b, wcat, b_a, b_x, lam, q_gain, group, reverse):
    b, s, w = xr.shape
    tq = min(TQ_ATTN, s)
    n_tiles = s // tq
    blocks8 = tq // SUBLANES
    gw = Q_GROUP * HEAD_DIM
    tile = (lambda i: n_tiles - 1 - i) if reverse else (lambda i: i)
    qspec = pl.BlockSpec((None, tq, gw), lambda bi, i: (bi, i, group))
    kvspec = pl.BlockSpec((None, s, HEAD_DIM), lambda bi, i: (bi, 0, group))
    main = pl.BlockSpec((None, tq, w), lambda bi, i: (bi, tile(i), 0))
    prev = pl.BlockSpec((None, SUBLANES, w),
                        lambda bi, i: (bi, jnp.maximum(tile(i) * blocks8 - 1, 0), 0))
    nxt = pl.BlockSpec((None, SUBLANES, w),
                       lambda bi, i: (bi, jnp.minimum((tile(i) + 1) * blocks8, s // SUBLANES - 1), 0))
    vec = _resident((1, w))
    return pl.pallas_call(
        functools.partial(_mixer_kernel, tq=tq, s_len=s, kv_chunk=min(KV_CHUNK, s), n_tiles=n_tiles,
                          reverse=reverse),
        grid=(b, n_tiles),
        in_specs=[qspec, kvspec, kvspec, main, prev, nxt, _resident((CONV_WIDTH, w)), vec,
                  _resident(wcat.shape), vec, vec, vec, _resident((1, HEAD_DIM))],
        out_specs=[pl.BlockSpec((None, tq, gw), lambda bi, i: (bi, i, 0)), main],
        out_shape=[jax.ShapeDtypeStruct((b, s, gw), BF16), jax.ShapeDtypeStruct((b, s, w), BF16)],
        scratch_shapes=[pltpu.VMEM((s, 2 * HEAD_DIM), BF16), pltpu.SMEM((1,), F32),
                        pltpu.VMEM((tq + 2 * SUBLANES, w), F32), pltpu.VMEM((SUBLANES, w), F32)],
        compiler_params=_params("parallel", "arbitrary"),
        name="mixer_bwd" if reverse else "mixer_fwd",
    )(q, k, v, xr, xr, xr, conv_w, conv_b.reshape(1, w), wcat, b_a.reshape(1, w), b_x.reshape(1, w),
      lam.reshape(1, w), q_gain.reshape(1, HEAD_DIM))


def _mixer(q, k, v, xr, conv_w, conv_b, w_a, b_a, w_x, b_x, lam, q_gain):
    wcat = jnp.concatenate([w_a, w_x], axis=-1).astype(BF16)
    outs = [_mixer_group(q, k, v, xr, conv_w, conv_b, wcat[g], b_a[g], b_x[g], lam[g], q_gain, g,
                         g == 1)
            for g in range(N_KV_HEADS)]
    (attn0, h_fwd), (attn1, h_bwd) = outs
    return attn0, attn1, h_fwd, h_bwd


def _merge_kernel(attn0_ref, attn1_ref, hf_ref, hb_ref, gxg_ref, sg_ref, x_ref, mod_ref, woa_ref,
                  wol_ref, wout_ref, o_ref, *, d_model):
    gw = attn0_ref.shape[-1]
    ya = (jnp.dot(attn0_ref[...], woa_ref[:gw, :], preferred_element_type=F32)
          + jnp.dot(attn1_ref[...], woa_ref[gw:, :], preferred_element_type=F32))
    ylru = (hf_ref[...].astype(F32) + hb_ref[...].astype(F32)) * gxg_ref[...].astype(F32)
    yl = jnp.dot(ylru.astype(BF16), wol_ref[...], preferred_element_type=F32)
    sg = sg_ref[...].astype(F32)
    merged = sg[:, :d_model] * ya + sg[:, d_model:] * yl
    out = jnp.dot(merged.astype(BF16), wout_ref[...], preferred_element_type=F32)
    o_ref[...] = x_ref[...] + mod_ref[...][2:3] * out


def _merge(attn0, attn1, h_fwd, h_bwd, gxg, sg, x, mod, w_o_attn, w_o_lru, w_out):
    b, s, d = x.shape
    tm = min(TM_MERGE, s)
    row = lambda w: pl.BlockSpec((None, tm, w), lambda bi, i: (bi, i, 0))
    gw = attn0.shape[-1]
    return pl.pallas_call(
        functools.partial(_merge_kernel, d_model=d),
        grid=(b, s // tm),
        in_specs=[row(gw), row(gw), row(d), row(d), row(d), row(2 * d), row(d),
                  pl.BlockSpec((None, N_MOD, d), lambda bi, i: (bi, 0, 0)),
                  _resident(w_o_attn.shape), _resident(w_o_lru.shape), _resident(w_out.shape)],
        out_specs=row(d),
        out_shape=jax.ShapeDtypeStruct(x.shape, F32),
        compiler_params=_params("parallel", "parallel"),
        name="merge",
    )(attn0, attn1, h_fwd, h_bwd, gxg, sg, x, mod, w_o_attn, w_o_lru, w_out)


def _swiglu_chunks(hb, wg_ref, wu_ref, wd_ref):
    d_ff = wg_ref.shape[-1]
    acc = None
    for lo in range(0, d_ff, FF_CHUNK):
        hi = min(lo + FF_CHUNK, d_ff)
        g = jnp.dot(hb, wg_ref[:, lo:hi], preferred_element_type=F32)
        u = jnp.dot(hb, wu_ref[:, lo:hi], preferred_element_type=F32)
        act = (g * jax.nn.sigmoid(g) * u).astype(BF16)
        y = jnp.dot(act, wd_ref[lo:hi, :], preferred_element_type=F32)
        acc = y if acc is None else acc + y
    return acc


def _final_norm(x, gain):
    return x * lax.rsqrt(jnp.mean(x * x, axis=-1, keepdims=True) + NORM_EPS) * gain


def _ffn_kernel(x_ref, mod_ref, wg_ref, wu_ref, wd_ref, fg_ref, o_ref, *, final):
    x = x_ref[...]
    mod = mod_ref[...]
    hb = _modulated_norm(x, mod[3:4], mod[4:5]).astype(BF16)
    y = x + mod[5:6] * _swiglu_chunks(hb, wg_ref, wu_ref, wd_ref)
    o_ref[...] = _final_norm(y, fg_ref[...]) if final else y


def _dense_ffn(x, mod, w_gate, w_up, w_down, final_gain, final):
    b, s, d = x.shape
    tm = min(TM_FFN, s)
    row = pl.BlockSpec((None, tm, d), lambda bi, i: (bi, i, 0))
    return pl.pallas_call(
        functools.partial(_ffn_kernel, final=final),
        grid=(b, s // tm),
        in_specs=[row, pl.BlockSpec((None, N_MOD, d), lambda bi, i: (bi, 0, 0)),
                  _resident(w_gate.shape), _resident(w_up.shape), _resident(w_down.shape),
                  _resident((1, d))],
        out_specs=row,
        out_shape=jax.ShapeDtypeStruct(x.shape, F32),
        compiler_params=_params("parallel", "parallel"),
        name="dense_ffn",
    )(x, mod, w_gate, w_up, w_down, final_gain.reshape(1, d))


R_E0, R_E1, R_RANK0, R_RANK1, R_W0, R_W1 = range(6)


def _router_kernel(x_ref, mod_ref, rw_ref, rb_ref, hp_ref, info_ref, cnt_ref, carry_scr, *, tm):
    @pl.when((pl.program_id(0) == 0) & (pl.program_id(1) == 0))
    def _():
        carry_scr[...] = jnp.zeros_like(carry_scr)

    mod = mod_ref[...]
    h = _modulated_norm(x_ref[...], mod[3:4], mod[4:5])
    hp_ref[...] = h

    logits = jnp.dot(h, rw_ref[...], preferred_element_type=F32,
                     precision=lax.Precision.HIGHEST) + rb_ref[...]
    lane = lax.broadcasted_iota(jnp.int32, logits.shape, 1)
    m1 = jnp.max(logits, axis=-1, keepdims=True)
    e0 = jnp.min(jnp.where(logits == m1, lane, LANES), axis=-1, keepdims=True)
    rest = jnp.where(lane == e0, -jnp.inf, logits)
    m2 = jnp.max(rest, axis=-1, keepdims=True)
    e1 = jnp.min(jnp.where(rest == m2, lane, LANES), axis=-1, keepdims=True)
    t = jnp.exp(m2 - m1)
    w0 = 1.0 / (1.0 + t)
    w1 = t * w0

    hit0 = lane == e0
    hit1 = lane == e1
    onehot = jnp.where(hit0 | hit1, 1.0, 0.0)
    ri = lax.broadcasted_iota(jnp.int32, (tm, tm), 0)
    ci = lax.broadcasted_iota(jnp.int32, (tm, tm), 1)
    before = jnp.where(ci < ri, 1.0, 0.0).astype(BF16)
    seen = jnp.dot(before, onehot.astype(BF16), preferred_element_type=F32) + carry_scr[0:1, :]
    rank0 = jnp.sum(jnp.where(hit0, seen, 0.0), axis=-1, keepdims=True)
    rank1 = jnp.sum(jnp.where(hit1, seen, 0.0), axis=-1, keepdims=True)
    carry = carry_scr[0:1, :] + jnp.sum(onehot, axis=0, keepdims=True)
    carry_scr[0:1, :] = carry
    cnt_ref[...] = jnp.broadcast_to(carry, cnt_ref.shape)

    info = jnp.zeros(logits.shape, F32)
    for col, val in ((R_E0, e0.astype(F32)), (R_E1, e1.astype(F32)), (R_RANK0, rank0),
                     (R_RANK1, rank1), (R_W0, w0), (R_W1, w1)):
        info = jnp.where(lane == col, val, info)
    info_ref[...] = info


def _router(x, mod, router_w, router_b):
    b, s, d = x.shape
    tm = min(TM_ROUTER, s)
    n_e = router_w.shape[-1]
    rw = jnp.zeros((d, LANES), F32).at[:, :n_e].set(router_w)
    rb = jnp.full((1, LANES), -jnp.inf, F32).at[0, :n_e].set(router_b)
    nt = s // tm
    return pl.pallas_call(
        functools.partial(_router_kernel, tm=tm),
        grid=(b, nt),
        in_specs=[pl.BlockSpec((None, tm, d), lambda bi, i: (bi, i, 0)),
                  pl.BlockSpec((None, N_MOD, d), lambda bi, i: (bi, 0, 0)),
                  _resident(rw.shape), _resident(rb.shape)],
        out_specs=[pl.BlockSpec((tm, d), lambda bi, i: (bi * nt + i, 0)),
                   pl.BlockSpec((tm, LANES), lambda bi, i: (bi * nt + i, 0)),
                   _resident((SUBLANES, LANES))],
        out_shape=[jax.ShapeDtypeStruct((b * s, d), F32),
                   jax.ShapeDtypeStruct((b * s, LANES), F32),
                   jax.ShapeDtypeStruct((SUBLANES, LANES), F32)],
        scratch_shapes=[pltpu.VMEM((SUBLANES, LANES), F32)],
        compiler_params=_params("arbitrary", "arbitrary"),
        name="router",
    )(x, mod, rw, rb)


SC_GATHER_WINDOW = 32


def _index_windows(idx):
    windows = idx.reshape(idx.shape[0] // SC_GATHER_WINDOW, SC_GATHER_WINDOW)
    return jnp.pad(windows, ((0, 0), (0, LANES - SC_GATHER_WINDOW)))


def _gather_rows(table, idx):
    m = idx.shape[0]
    d = table.shape[1]
    mesh = plsc.VectorSubcoreMesh(core_axis_name="core", subcore_axis_name="subcore")

    @functools.partial(pl.kernel, out_type=jax.ShapeDtypeStruct((m, d), table.dtype), mesh=mesh,
                       scratch_types=[])
    def gather(table_hbm, idx_hbm, out_hbm):
        def body(idx_vmem, out_vmem):
            pltpu.sync_copy(table_hbm.at[idx_vmem.at[0, pl.ds(0, SC_GATHER_WINDOW)]], out_vmem)

        pltpu.emit_pipeline(
            body,
            grid=(m // SC_GATHER_WINDOW,),
            in_specs=[pl.BlockSpec((1, LANES), index_map=lambda i: (i, 0))],
            out_specs=[pl.BlockSpec((SC_GATHER_WINDOW, d), index_map=lambda i: (i, 0))],
            core_axis_name=("core", "subcore"),
            dimension_semantics=(pltpu.PARALLEL,),
        )(idx_hbm, out_hbm)

    return gather(table, _index_windows(idx))


def _scatter_rows(sources, n_rows):
    d = sources[0][0].shape[1]
    mesh = plsc.VectorSubcoreMesh(core_axis_name="core", subcore_axis_name="subcore")

    @functools.partial(pl.kernel, out_type=jax.ShapeDtypeStruct((n_rows, d), sources[0][0].dtype),
                       mesh=mesh, scratch_types=[])
    def scatter(*refs):
        out_hbm = refs[-1]

        def body(rows_vmem, idx_vmem):
            pltpu.sync_copy(rows_vmem, out_hbm.at[idx_vmem.at[0, pl.ds(0, SC_GATHER_WINDOW)]])

        for t, (table, _) in enumerate(sources):
            pltpu.emit_pipeline(
                body,
                grid=(table.shape[0] // SC_GATHER_WINDOW,),
                in_specs=[pl.BlockSpec((SC_GATHER_WINDOW, d), index_map=lambda i: (i, 0)),
                          pl.BlockSpec((1, LANES), index_map=lambda i: (i, 0))],
                out_specs=[],
                core_axis_name=("core", "subcore"),
                dimension_semantics=(pltpu.PARALLEL,),
            )(refs[2 * t], refs[2 * t + 1])

    args = []
    for table, idx in sources:
        args += [table, _index_windows(idx)]
    return scatter(*args)


def _expert_kernel(te_ref, nu_ref, xs_ref, wg_ref, wu_ref, wd_ref, y_ref):
    del te_ref
    i = pl.program_id(0)

    @pl.when(i < nu_ref[0])
    def _():
        y_ref[...] = _swiglu_chunks(xs_ref[...].astype(BF16), wg_ref, wu_ref, wd_ref)

    @pl.when(i >= nu_ref[0])
    def _():
        y_ref[...] = jnp.zeros_like(y_ref)


def _grouped_experts(xs, tile_expert, n_used, w_gate, w_up, w_down):
    n_rows, d = xs.shape
    tm = TM_EXPERT
    ff = w_gate.shape[-1]
    grid_spec = pltpu.PrefetchScalarGridSpec(
        num_scalar_prefetch=2,
        grid=(n_rows // tm,),
        in_specs=[pl.BlockSpec((tm, d), lambda i, te, nu: (i, 0)),
                  pl.BlockSpec((None, d, ff), lambda i, te, nu: (te[i], 0, 0)),
                  pl.BlockSpec((None, d, ff), lambda i, te, nu: (te[i], 0, 0)),
                  pl.BlockSpec((None, ff, d), lambda i, te, nu: (te[i], 0, 0))],
        out_specs=pl.BlockSpec((tm, d), lambda i, te, nu: (i, 0)),
    )
    return pl.pallas_call(
        _expert_kernel,
        grid_spec=grid_spec,
        out_shape=jax.ShapeDtypeStruct((n_rows, d), F32),
        compiler_params=pltpu.CompilerParams(dimension_semantics=("arbitrary",),
                                             vmem_limit_bytes=60 * 1024 * 1024),
        name="moe_experts",
    )(tile_expert, n_used, xs, w_gate, w_up, w_down)


def _combine_kernel(rows0_ref, rows1_ref, x_ref, mod_ref, info_ref, fg_ref, o_ref, *, final):
    info = info_ref[...]
    ff = info[:, R_W0:R_W0 + 1] * rows0_ref[...] + info[:, R_W1:R_W1 + 1] * rows1_ref[...]
    y = x_ref[...] + mod_ref[...][5:6] * ff
    o_ref[...] = _final_norm(y, fg_ref[...]) if final else y


def _combine(y_sorted, pos, x, mod, info, final_gain, final):
    b, s, d = x.shape
    tm = min(TM_COMBINE, s)
    nt = s // tm
    rows = _gather_rows(y_sorted, pos.reshape(b * s, TOP_K).T.reshape(-1))
    second = (b * s) // tm
    return pl.pallas_call(
        functools.partial(_combine_kernel, final=final),
        grid=(b, nt),
        in_specs=[pl.BlockSpec((tm, d), lambda bi, i: (bi * nt + i, 0)),
                  pl.BlockSpec((tm, d), lambda bi, i: (second + bi * nt + i, 0)),
                  pl.BlockSpec((None, tm, d), lambda bi, i: (bi, i, 0)),
                  pl.BlockSpec((None, N_MOD, d), lambda bi, i: (bi, 0, 0)),
                  pl.BlockSpec((tm, LANES), lambda bi, i: (bi * nt + i, 0)),
                  _resident((1, d))],
        out_specs=pl.BlockSpec((None, tm, d), lambda bi, i: (bi, i, 0)),
        out_shape=jax.ShapeDtypeStruct(x.shape, F32),
        compiler_params=_params("parallel", "parallel"),
        name="moe_combine",
    )(rows, rows, x, mod, info, final_gain.reshape(1, d))


def _moe_ffn(x, mod, router_w, router_b, w_gate, w_up, w_down, final_gain, final):
    b, s, d = x.shape
    n = b * s
    n_e = router_w.shape[-1]
    tm = TM_EXPERT
    hp, info, cnt = _router(x, mod, router_w, router_b)

    counts = cnt[0, :n_e].astype(jnp.int32)
    tiles = (counts + tm - 1) // tm
    tile_end = jnp.cumsum(tiles)
    row_start = (tile_end - tiles) * tm
    n_tiles = (TOP_K * n) // tm + n_e
    tile_expert = jnp.minimum(
        jnp.sum(jnp.arange(n_tiles, dtype=jnp.int32)[:, None] >= tile_end[None, :], axis=1),
        n_e - 1).astype(jnp.int32)
    n_used = tile_end[-1:].astype(jnp.int32)
    experts = info[:, R_E0:R_E1 + 1].astype(jnp.int32)
    ranks = info[:, R_RANK0:R_RANK1 + 1].astype(jnp.int32)
    pos2 = row_start[experts] + ranks
    pos = pos2.reshape(-1)
    n_fill = n_e * tm
    fill = tiles * tm - counts
    fill_end = jnp.cumsum(fill)
    j = jnp.arange(n_fill, dtype=jnp.int32)
    group = jnp.sum(j[:, None] >= fill_end[None, :], axis=1)
    first = jnp.concatenate([row_start + counts, tile_end[-1:] * tm])[group]
    before = jnp.concatenate([jnp.zeros((1,), jnp.int32), fill_end])[group]
    fill_pos = first + j - before

    xs = _scatter_rows([(hp, pos2[:, k]) for k in range(TOP_K)]
                       + [(jnp.zeros((n_fill, d), hp.dtype), fill_pos)], n_tiles * tm)
    y_sorted = _grouped_experts(xs, tile_expert, n_used, w_gate, w_up, w_down)
    return _combine(y_sorted, pos, x, mod, info, final_gain, final)


def _rope_tables(seq_len):
    pos = jnp.arange(seq_len, dtype=jnp.int32)
    axis_pos = jnp.stack([pos // GRID_W, pos % GRID_W], axis=1).astype(F32)
    n_freq = HEAD_DIM // 4
    inv_freq = jnp.exp(-math.log(ROPE_THETA) * (2.0 * jnp.arange(n_freq, dtype=F32) / (HEAD_DIM // 2)))
    ang = axis_pos[:, :, None] * inv_freq[None, None, :]
    cos = jnp.cos(ang)
    sin = jnp.sin(ang)
    cos_t = jnp.concatenate([cos, cos], axis=-1).reshape(seq_len, HEAD_DIM)
    sin_t = jnp.concatenate([-sin, sin], axis=-1).reshape(seq_len, HEAD_DIM)
    return cos_t, sin_t


def kernel(x, c, w_mod, b_mod, w_in, q_norm_gain, k_norm_gain, conv_w, conv_b, lru_w_a, lru_b_a,
           lru_w_x, lru_b_x, lru_lambda, w_o_attn, w_o_lru, w_out, ffn_w_gate, ffn_w_up,
           ffn_w_down, router_w, router_b, moe_w_gate, moe_w_up, moe_w_down, final_gain):
    b, s, d = x.shape
    depth = w_in.shape[0]
    cos, sin = _rope_tables(s)
    mods = _modulation(c, w_mod, b_mod).reshape(depth, b, N_MOD, d)
    for l in range(depth):
        mod = mods[l]
        q, k, v, xr, gxg, sg = _inproj(x, mod, w_in[l].astype(BF16), q_norm_gain[l], k_norm_gain[l],
                                       cos, sin)
        attn0, attn1, h_fwd, h_bwd = _mixer(q, k, v, xr, conv_w[l], conv_b[l], lru_w_a[l], lru_b_a[l],
                                            lru_w_x[l], lru_b_x[l], lru_lambda[l], q_norm_gain[l])
        x = _merge(attn0, attn1, h_fwd, h_bwd, gxg, sg, x, mod, w_o_attn[l].astype(BF16),
                   w_o_lru[l].astype(BF16), w_out[l].astype(BF16))
        final = l == depth - 1
        j = l // 2
        if l % 2 == 0:
            x = _dense_ffn(x, mod, ffn_w_gate[j].astype(BF16), ffn_w_up[j].astype(BF16),
                           ffn_w_down[j].astype(BF16), final_gain, final)
        else:
            x = _moe_ffn(x, mod, router_w[j], router_b[j], moe_w_gate[j].astype(BF16),
                         moe_w_up[j].astype(BF16), moe_w_down[j].astype(BF16), final_gain, final)
    return x
```

```python
import functools
import math

import jax
import jax.numpy as jnp
from jax import lax
from jax.experimental import pallas as pl
from jax.experimental.pallas import tpu as pltpu
from jax.experimental.pallas import tpu_sc as plsc

HEAD_DIM = 128
N_Q_HEADS = 8
N_KV_HEADS = 2
Q_GROUP = N_Q_HEADS // N_KV_HEADS
GRID_W = 64
ROPE_THETA = 10000.0
ROPE_PAIR_SHIFT = HEAD_DIM // 4
LRU_BLOCKS = 8
LRU_C = 8.0
CONV_WIDTH = 4
N_MOD = 6
TOP_K = 2
NORM_EPS = 1e-6
LOG2_E = math.log2(math.e)
SOFTMAX_SAFE_LOG2_SHIFT = 55.0
BF16_ROUNDING_MARGIN = 1.0 + 2.0 ** -7

LANES = 128
SUBLANES = 8
VMEM_LIMIT_BYTES = 56 * 1024 * 1024

TM_INPROJ = 256
TQ_ATTN = 256
TM_MERGE = 512
TM_FFN = 512
TM_ROUTER = 256
TM_EXPERT = 512
TM_COMBINE = 256
FF_CHUNK = 512
KV_CHUNK = 512

BF16 = jnp.bfloat16
F32 = jnp.float32


def _params(*semantics):
    return pltpu.CompilerParams(dimension_semantics=semantics, vmem_limit_bytes=VMEM_LIMIT_BYTES)


def _resident(shape):
    zeros = (0,) * len(shape)
    return pl.BlockSpec(shape, lambda *_: zeros, pipeline_mode=pl.Buffered(1))


def _modulated_norm(x, shift, scale):
    ms = jnp.mean(x * x, axis=-1, keepdims=True)
    return x * lax.rsqrt(ms + NORM_EPS) * (1.0 + scale) + shift


def _mod_kernel(c_ref, w_ref, b_ref, o_ref):
    c = c_ref[...]
    act = c * jax.nn.sigmoid(c)
    o_ref[...] = jnp.dot(act, w_ref[...], preferred_element_type=F32,
                         precision=lax.Precision.HIGHEST) + b_ref[...]


def _modulation(c, w_mod, b_mod):
    depth, d, n = w_mod.shape
    b = c.shape[0]
    tn = 1536
    return pl.pallas_call(
        _mod_kernel,
        grid=(depth, n // tn),
        in_specs=[pl.BlockSpec((b, d), lambda l, j: (0, 0)),
                  pl.BlockSpec((None, d, tn), lambda l, j: (l, 0, j)),
                  pl.BlockSpec((None, 1, tn), lambda l, j: (l, 0, j))],
        out_specs=pl.BlockSpec((None, b, tn), lambda l, j: (l, 0, j)),
        out_shape=jax.ShapeDtypeStruct((depth, b, n), F32),
        compiler_params=_params("parallel", "parallel"),
        name="modulation",
    )(c, w_mod, b_mod.reshape(depth, 1, n))


def _inproj_kernel(x_ref, mod_ref, w_ref, qg_ref, kg_ref, cos_ref, sin_ref,
                   q_ref, k_ref, v_ref, xr_ref, gxg_ref, sg_ref, *, d_model):
    attn_w = N_Q_HEADS * HEAD_DIM
    kv_w = N_KV_HEADS * HEAD_DIM
    mod = mod_ref[...]
    hb = _modulated_norm(x_ref[...], mod[0:1], mod[1:2]).astype(BF16)

    cos = cos_ref[...]
    sin = sin_ref[...]
    lane = lax.broadcasted_iota(jnp.int32, cos.shape, 1)
    pair_first = (lane & ROPE_PAIR_SHIFT) == 0

    def norm_rope(y, gain, post_scale):
        r = lax.rsqrt(jnp.mean(y * y, axis=-1, keepdims=True) + NORM_EPS)
        yn = y * r * gain
        partner = jnp.where(pair_first,
                            pltpu.roll(yn, HEAD_DIM - ROPE_PAIR_SHIFT, 1),
                            pltpu.roll(yn, ROPE_PAIR_SHIFT, 1))
        return (yn * cos + partner * sin) * post_scale

    def proj(lo, width):
        return jnp.dot(hb, w_ref[:, lo:lo + width], preferred_element_type=F32)

    qf = proj(0, attn_w)
    qg = qg_ref[...]
    for h in range(N_Q_HEADS):
        sl = slice(h * HEAD_DIM, (h + 1) * HEAD_DIM)
        q_ref[:, sl] = norm_rope(qf[:, sl], qg, HEAD_DIM ** -0.5 * LOG2_E).astype(BF16)
    kf = proj(attn_w, kv_w)
    kg = kg_ref[...]
    for h in range(N_KV_HEADS):
        sl = slice(h * HEAD_DIM, (h + 1) * HEAD_DIM)
        k_ref[:, sl] = norm_rope(kf[:, sl], kg, 1.0).astype(BF16)
    v_ref[...] = proj(attn_w + kv_w, kv_w).astype(BF16)
    lo = attn_w + 2 * kv_w
    xr_ref[...] = proj(lo, d_model)
    gxg_ref[...] = jax.nn.gelu(proj(lo + d_model, d_model), approximate=True).astype(BF16)
    sg_ref[...] = jax.nn.sigmoid(proj(lo + 2 * d_model, 2 * d_model)).astype(BF16)


def _inproj(x, mod, w_in, q_gain, k_gain, cos, sin):
    b, s, d = x.shape
    tm = min(TM_INPROJ, s)
    attn_w = N_Q_HEADS * HEAD_DIM
    kv_w = N_KV_HEADS * HEAD_DIM
    row = lambda w: pl.BlockSpec((None, tm, w), lambda bi, i: (bi, i, 0))
    tab = pl.BlockSpec((tm, HEAD_DIM), lambda bi, i: (i, 0))
    outs = [(attn_w, BF16), (kv_w, BF16), (kv_w, BF16), (d, F32), (d, BF16), (2 * d, BF16)]
    return pl.pallas_call(
        functools.partial(_inproj_kernel, d_model=d),
        grid=(b, s // tm),
        in_specs=[row(d),
                  pl.BlockSpec((None, N_MOD, d), lambda bi, i: (bi, 0, 0)),
                  _resident(w_in.shape), _resident((1, HEAD_DIM)), _resident((1, HEAD_DIM)),
                  tab, tab],
        out_specs=[row(w) for w, _ in outs],
        out_shape=[jax.ShapeDtypeStruct((b, s, w), dt) for w, dt in outs],
        compiler_params=_params("parallel", "parallel"),
        name="inproj",
    )(x, mod, w_in, q_gain.reshape(1, HEAD_DIM), k_gain.reshape(1, HEAD_DIM), cos, sin)


def _lru_stage(xr_ref, prev_ref, next_ref, xs_scr, *, ts, tile, n_tiles):
    xs_scr[0:SUBLANES, :] = jnp.where(tile > 0, prev_ref[...], 0.0)
    xs_scr[SUBLANES:SUBLANES + ts, :] = xr_ref[...]
    xs_scr[SUBLANES + ts:, :] = jnp.where(tile < n_tiles - 1, next_ref[...], 0.0)


def _lru_block(n, cw_ref, cb_ref, wcat_ref, ba_ref, bx_ref, lam_ref, h_ref, xs_scr, carry_scr,
               *, ts, reverse):
    bw = LANES
    sl = slice(n * bw, (n + 1) * bw)

    def tap(shift):
        return xs_scr[SUBLANES + shift:SUBLANES + shift + ts, sl]

    cw = cw_ref[:, sl]
    xb = (cw[0:1] * tap(-2) + cw[1:2] * tap(-1) + cw[2:3] * tap(0) + cw[3:4] * tap(1)
          + cb_ref[:, sl])
    lam = lam_ref[:, sl]
    neg_c_softplus = -LRU_C * (jnp.maximum(-lam, 0.0) + jnp.log1p(jnp.exp(-jnp.abs(lam))))
    g = jnp.dot(xb.astype(BF16), wcat_ref[n], preferred_element_type=F32)
    r = jax.nn.sigmoid(g[:, :bw] + ba_ref[:, sl])
    i = jax.nn.sigmoid(g[:, bw:] + bx_ref[:, sl])
    log_a = neg_c_softplus * r
    a_all = jnp.exp(log_a)
    th = jnp.tanh(log_a)
    u_all = jnp.exp(0.5 * jnp.log(-2.0 * th / (1.0 - th))) * (i * xb)

    row = lax.broadcasted_iota(jnp.int32, (SUBLANES, bw), 0)
    n_groups = ts // SUBLANES
    state = carry_scr[0:1, sl]
    states = [None] * n_groups
    for v in (range(n_groups - 1, -1, -1) if reverse else range(n_groups)):
        rows = slice(v * SUBLANES, (v + 1) * SUBLANES)
        a = a_all[rows]
        u = u_all[rows]
        for d in (1, 2, 4):
            keep = (row < SUBLANES - d) if reverse else (row >= d)
            shift = (SUBLANES - d) if reverse else d
            a_prev = jnp.where(keep, pltpu.roll(a, shift, 0), 1.0)
            u_prev = jnp.where(keep, pltpu.roll(u, shift, 0), 0.0)
            u = a * u_prev + u
            a = a * a_prev
        h = u + a * state
        states[v] = h
        state = h[0:1] if reverse else h[SUBLANES - 1:SUBLANES]
    carry_scr[0:1, sl] = state
    h_ref[:, sl] = jnp.concatenate(states, axis=0).astype(BF16)


def _mixer_kernel(q_ref, k_ref, v_ref, xr_ref, prev_ref, next_ref, cw_ref, cb_ref, wcat_ref, ba_ref,
                  bx_ref, lam_ref, qg_ref, o_ref, h_ref, vext_scr, shift_scr, xs_scr, carry_scr,
                  *, tq, s_len, kv_chunk, n_tiles, reverse):
    nt = (((1,), (1,)), ((), ()))
    i = pl.program_id(1)

    @pl.when(i == 0)
    def _():
        carry_scr[...] = jnp.zeros_like(carry_scr)
        vext_scr[:, :HEAD_DIM] = v_ref[...]
        vext_scr[:, HEAD_DIM:] = jnp.ones((s_len, HEAD_DIM), BF16)
        kf = k_ref[...].astype(F32)
        k_norm_max = jnp.max(jnp.sqrt(jnp.sum(kf * kf, axis=-1, keepdims=True)))
        q_norm_max = jnp.max(jnp.abs(qg_ref[...])) * (LOG2_E * BF16_ROUNDING_MARGIN)
        shift_scr[0] = q_norm_max * k_norm_max

    lru_stage = functools.partial(_lru_stage, xr_ref, prev_ref, next_ref, xs_scr, ts=tq,
                                  tile=(n_tiles - 1 - i) if reverse else i, n_tiles=n_tiles)
    lru_block = functools.partial(_lru_block, cw_ref=cw_ref, cb_ref=cb_ref, wcat_ref=wcat_ref,
                                  ba_ref=ba_ref, bx_ref=bx_ref, lam_ref=lam_ref, h_ref=h_ref,
                                  xs_scr=xs_scr, carry_scr=carry_scr, ts=tq, reverse=reverse)

    q = q_ref[...]
    qs = jnp.concatenate([q[:, h * HEAD_DIM:(h + 1) * HEAD_DIM] for h in range(Q_GROUP)], axis=0)
    shift = shift_scr[0]
    safe = shift <= SOFTMAX_SAFE_LOG2_SHIFT

    def finish(acc):
        o = acc[:, :HEAD_DIM] / acc[:, HEAD_DIM:]
        for h in range(Q_GROUP):
            o_ref[:, h * HEAD_DIM:(h + 1) * HEAD_DIM] = o[h * tq:(h + 1) * tq].astype(BF16)

    @pl.when(safe)
    def _():
        lru_stage()
        chunks = list(range(0, s_len, kv_chunk))
        per_chunk = -(-LRU_BLOCKS // len(chunks))
        acc = None
        for c, lo in enumerate(chunks):
            s = lax.dot_general(qs, k_ref[lo:lo + kv_chunk, :], nt, preferred_element_type=F32)
            p = jnp.exp2(s - shift).astype(BF16)
            pv = jnp.dot(p, vext_scr[lo:lo + kv_chunk, :], preferred_element_type=F32)
            acc = pv if acc is None else acc + pv
            for n in range(c * per_chunk, min((c + 1) * per_chunk, LRU_BLOCKS)):
                lru_block(n)
        finish(acc)

    @pl.when(jnp.logical_not(safe))
    def _():
        s = lax.dot_general(qs, k_ref[...], nt, preferred_element_type=F32)
        p = jnp.exp2(s - jnp.max(s, axis=-1, keepdims=True)).astype(BF16)
        finish(jnp.dot(p, vext_scr[...], preferred_element_type=F32))
        lru_stage()
        for n in range(LRU_BLOCKS):
            lru_block(n)


def _mixer_group(q, k, v, xr, conv_w, conv_b, wcat, b_a, b_x, lam, q_gain, group, reverse):
    b, s, w = xr.shape
    tq = min(TQ_ATTN, s)
    n_tiles = s // tq
    blocks8 = tq // SUBLANES
    gw = Q_GROUP * HEAD_DIM
    tile = (lambda i: n_tiles - 1 - i) if reverse else (lambda i: i)
    qspec = pl.BlockSpec((None, tq, gw), lambda bi, i: (bi, i, group))
    kvspec = pl.BlockSpec((None, s, HEAD_DIM), lambda bi, i: (bi, 0, group))
    main = pl.BlockSpec((None, tq, w), lambda bi, i: (bi, tile(i), 0))
    prev = pl.BlockSpec((None, SUBLANES, w),
                        lambda bi, i: (bi, jnp.maximum(tile(i) * blocks8 - 1, 0), 0))
    nxt = pl.BlockSpec((None, SUBLANES, w),
                       lambda bi, i: (bi, jnp.minimum((tile(i) + 1) * blocks8, s // SUBLANES - 1), 0))
    vec = _resident((1, w))
    return pl.pallas_call(
        functools.partial(_mixer_kernel, tq=tq, s_len=s, kv_chunk=min(KV_CHUNK, s), n_tiles=n_tiles,
                          reverse=reverse),
        grid=(b, n_tiles),
        in_specs=[qspec, kvspec, kvspec, main, prev, nxt, _resident((CONV_WIDTH, w)), vec,
                  _resident(wcat.shape), vec, vec, vec, _resident((1, HEAD_DIM))],
        out_specs=[pl.BlockSpec((None, tq, gw), lambda bi, i: (bi, i, 0)), main],
        out_shape=[jax.ShapeDtypeStruct((b, s, gw), BF16), jax.ShapeDtypeStruct((b, s, w), BF16)],
        scratch_shapes=[pltpu.VMEM((s, 2 * HEAD_DIM), BF16), pltpu.SMEM((1,), F32),
                        pltpu.VMEM((tq + 2 * SUBLANES, w), F32), pltpu.VMEM((SUBLANES, w), F32)],
        compiler_params=_params("parallel", "arbitrary"),
        name="mixer_bwd" if reverse else "mixer_fwd",
    )(q, k, v, xr, xr, xr, conv_w, conv_b.reshape(1, w), wcat, b_a.reshape(1, w), b_x.reshape(1, w),
      lam.reshape(1, w), q_gain.reshape(1, HEAD_DIM))


def _mixer(q, k, v, xr, conv_w, conv_b, w_a, b_a, w_x, b_x, lam, q_gain):
    wcat = jnp.concatenate([w_a, w_x], axis=-1).astype(BF16)
    outs = [_mixer_group(q, k, v, xr, conv_w, conv_b, wcat[g], b_a[g], b_x[g], lam[g], q_gain, g,
                         g == 1)
            for g in range(N_KV_HEADS)]
    (attn0, h_fwd), (attn1, h_bwd) = outs
    return attn0, attn1, h_fwd, h_bwd


def _merge_kernel(attn0_ref, attn1_ref, hf_ref, hb_ref, gxg_ref, sg_ref, x_ref, mod_ref, woa_ref,
                  wol_ref, wout_ref, o_ref, *, d_model):
    gw = attn0_ref.shape[-1]
    ya = (jnp.dot(attn0_ref[...], woa_ref[:gw, :], preferred_element_type=F32)
          + jnp.dot(attn1_ref[...], woa_ref[gw:, :], preferred_element_type=F32))
    ylru = (hf_ref[...].astype(F32) + hb_ref[...].astype(F32)) * gxg_ref[...].astype(F32)
    yl = jnp.dot(ylru.astype(BF16), wol_ref[...], preferred_element_type=F32)
    sg = sg_ref[...].astype(F32)
    merged = sg[:, :d_model] * ya + sg[:, d_model:] * yl
    out = jnp.dot(merged.astype(BF16), wout_ref[...], preferred_element_type=F32)
    o_ref[...] = x_ref[...] + mod_ref[...][2:3] * out


def _merge(attn0, attn1, h_fwd, h_bwd, gxg, sg, x, mod, w_o_attn, w_o_lru, w_out):
    b, s, d = x.shape
    tm = min(TM_MERGE, s)
    row = lambda w: pl.BlockSpec((None, tm, w), lambda bi, i: (bi, i, 0))
    gw = attn0.shape[-1]
    return pl.pallas_call(
        functools.partial(_merge_kernel, d_model=d),
        grid=(b, s // tm),
        in_specs=[row(gw), row(gw), row(d), row(d), row(d), row(2 * d), row(d),
                  pl.BlockSpec((None, N_MOD, d), lambda bi, i: (bi, 0, 0)),
                  _resident(w_o_attn.shape), _resident(w_o_lru.shape), _resident(w_out.shape)],
        out_specs=row(d),
        out_shape=jax.ShapeDtypeStruct(x.shape, F32),
        compiler_params=_params("parallel", "parallel"),
        name="merge",
    )(attn0, attn1, h_fwd, h_bwd, gxg, sg, x, mod, w_o_attn, w_o_lru, w_out)


def _swiglu_chunks(hb, wg_ref, wu_ref, wd_ref):
    d_ff = wg_ref.shape[-1]
    acc = None
    for lo in range(0, d_ff, FF_CHUNK):
        hi = min(lo + FF_CHUNK, d_ff)
        g = jnp.dot(hb, wg_ref[:, lo:hi], preferred_element_type=F32)
        u = jnp.dot(hb, wu_ref[:, lo:hi], preferred_element_type=F32)
        act = (g * jax.nn.sigmoid(g) * u).astype(BF16)
        y = jnp.dot(act, wd_ref[lo:hi, :], preferred_element_type=F32)
        acc = y if acc is None else acc + y
    return acc


def _final_norm(x, gain):
    return x * lax.rsqrt(jnp.mean(x * x, axis=-1, keepdims=True) + NORM_EPS) * gain


def _ffn_kernel(x_ref, mod_ref, wg_ref, wu_ref, wd_ref, fg_ref, o_ref, *, final):
    x = x_ref[...]
    mod = mod_ref[...]
    hb = _modulated_norm(x, mod[3:4], mod[4:5]).astype(BF16)
    y = x + mod[5:6] * _swiglu_chunks(hb, wg_ref, wu_ref, wd_ref)
    o_ref[...] = _final_norm(y, fg_ref[...]) if final else y


def _dense_ffn(x, mod, w_gate, w_up, w_down, final_gain, final):
    b, s, d = x.shape
    tm = min(TM_FFN, s)
    row = pl.BlockSpec((None, tm, d), lambda bi, i: (bi, i, 0))
    return pl.pallas_call(
        functools.partial(_ffn_kernel, final=final),
        grid=(b, s // tm),
        in_specs=[row, pl.BlockSpec((None, N_MOD, d), lambda bi, i: (bi, 0, 0)),
                  _resident(w_gate.shape), _resident(w_up.shape), _resident(w_down.shape),
                  _resident((1, d))],
        out_specs=row,
        out_shape=jax.ShapeDtypeStruct(x.shape, F32),
        compiler_params=_params("parallel", "parallel"),
        name="dense_ffn",
    )(x, mod, w_gate, w_up, w_down, final_gain.reshape(1, d))


R_E0, R_E1, R_RANK0, R_RANK1, R_W0, R_W1 = range(6)


def _router_kernel(x_ref, mod_ref, rw_ref, rb_ref, hp_ref, info_ref, cnt_ref, carry_scr, *, tm):
    @pl.when((pl.program_id(0) == 0) & (pl.program_id(1) == 0))
    def _():
        carry_scr[...] = jnp.zeros_like(carry_scr)

    mod = mod_ref[...]
    h = _modulated_norm(x_ref[...], mod[3:4], mod[4:5])
    hp_ref[...] = h

    logits = jnp.dot(h, rw_ref[...], preferred_element_type=F32,
                     precision=lax.Precision.HIGHEST) + rb_ref[...]
    lane = lax.broadcasted_iota(jnp.int32, logits.shape, 1)
    m1 = jnp.max(logits, axis=-1, keepdims=True)
    e0 = jnp.min(jnp.where(logits == m1, lane, LANES), axis=-1, keepdims=True)
    rest = jnp.where(lane == e0, -jnp.inf, logits)
    m2 = jnp.max(rest, axis=-1, keepdims=True)
    e1 = jnp.min(jnp.where(rest == m2, lane, LANES), axis=-1, keepdims=True)
    t = jnp.exp(m2 - m1)
    w0 = 1.0 / (1.0 + t)
    w1 = t * w0

    hit0 = lane == e0
    hit1 = lane == e1
    onehot = jnp.where(hit0 | hit1, 1.0, 0.0)
    ri = lax.broadcasted_iota(jnp.int32, (tm, tm), 0)
    ci = lax.broadcasted_iota(jnp.int32, (tm, tm), 1)
    before = jnp.where(ci < ri, 1.0, 0.0).astype(BF16)
    seen = jnp.dot(before, onehot.astype(BF16), preferred_element_type=F32) + carry_scr[0:1, :]
    rank0 = jnp.sum(jnp.where(hit0, seen, 0.0), axis=-1, keepdims=True)
    rank1 = jnp.sum(jnp.where(hit1, seen, 0.0), axis=-1, keepdims=True)
    carry = carry_scr[0:1, :] + jnp.sum(onehot, axis=0, keepdims=True)
    carry_scr[0:1, :] = carry
    cnt_ref[...] = jnp.broadcast_to(carry, cnt_ref.shape)

    info = jnp.zeros(logits.shape, F32)
    for col, val in ((R_E0, e0.astype(F32)), (R_E1, e1.astype(F32)), (R_RANK0, rank0),
                     (R_RANK1, rank1), (R_W0, w0), (R_W1, w1)):
        info = jnp.where(lane == col, val, info)
    info_ref[...] = info


def _router(x, mod, router_w, router_b):
    b, s, d = x.shape
    tm = min(TM_ROUTER, s)
    n_e = router_w.shape[-1]
    rw = jnp.zeros((d, LANES), F32).at[:, :n_e].set(router_w)
    rb = jnp.full((1, LANES), -jnp.inf, F32).at[0, :n_e].set(router_b)
    nt = s // tm
    return pl.pallas_call(
        functools.partial(_router_kernel, tm=tm),
        grid=(b, nt),
        in_specs=[pl.BlockSpec((None, tm, d), lambda bi, i: (bi, i, 0)),
                  pl.BlockSpec((None, N_MOD, d), lambda bi, i: (bi, 0, 0)),
                  _resident(rw.shape), _resident(rb.shape)],
        out_specs=[pl.BlockSpec((tm, d), lambda bi, i: (bi * nt + i, 0)),
                   pl.BlockSpec((tm, LANES), lambda bi, i: (bi * nt + i, 0)),
                   _resident((SUBLANES, LANES))],
        out_shape=[jax.ShapeDtypeStruct((b * s, d), F32),
                   jax.ShapeDtypeStruct((b * s, LANES), F32),
                   jax.ShapeDtypeStruct((SUBLANES, LANES), F32)],
        scratch_shapes=[pltpu.VMEM((SUBLANES, LANES), F32)],
        compiler_params=_params("arbitrary", "arbitrary"),
        name="router",
    )(x, mod, rw, rb)


SC_GATHER_WINDOW = 32


def _index_windows(idx):
    windows = idx.reshape(idx.shape[0] // SC_GATHER_WINDOW, SC_GATHER_WINDOW)
    return jnp.pad(windows, ((0, 0), (0, LANES - SC_GATHER_WINDOW)))


def _gather_rows(table, idx):
    m = idx.shape[0]
    d = table.shape[1]
    mesh = plsc.VectorSubcoreMesh(core_axis_name="core", subcore_axis_name="subcore")

    @functools.partial(pl.kernel, out_type=jax.ShapeDtypeStruct((m, d), table.dtype), mesh=mesh,
                       scratch_types=[])
    def gather(table_hbm, idx_hbm, out_hbm):
        def body(idx_vmem, out_vmem):
            pltpu.sync_copy(table_hbm.at[idx_vmem.at[0, pl.ds(0, SC_GATHER_WINDOW)]], out_vmem)

        pltpu.emit_pipeline(
            body,
            grid=(m // SC_GATHER_WINDOW,),
            in_specs=[pl.BlockSpec((1, LANES), index_map=lambda i: (i, 0))],
            out_specs=[pl.BlockSpec((SC_GATHER_WINDOW, d), index_map=lambda i: (i, 0))],
            core_axis_name=("core", "subcore"),
            dimension_semantics=(pltpu.PARALLEL,),
        )(idx_hbm, out_hbm)

    return gather(table, _index_windows(idx))


def _scatter_rows(sources, n_rows):
    d = sources[0][0].shape[1]
    mesh = plsc.VectorSubcoreMesh(core_axis_name="core", subcore_axis_name="subcore")

    @functools.partial(pl.kernel, out_type=jax.ShapeDtypeStruct((n_rows, d), sources[0][0].dtype),
                       mesh=mesh, scratch_types=[])
    def scatter(*refs):
        out_hbm = refs[-1]

        def body(rows_vmem, idx_vmem):
            pltpu.sync_copy(rows_vmem, out_hbm.at[idx_vmem.at[0, pl.ds(0, SC_GATHER_WINDOW)]])

        for t, (table, _) in enumerate(sources):
            pltpu.emit_pipeline(
                body,
                grid=(table.shape[0] // SC_GATHER_WINDOW,),
                in_specs=[pl.BlockSpec((SC_GATHER_WINDOW, d), index_map=lambda i: (i, 0)),
                          pl.BlockSpec((1, LANES), index_map=lambda i: (i, 0))],
                out_specs=[],
                core_axis_name=("core", "subcore"),
                dimension_semantics=(pltpu.PARALLEL,),
            )(refs[2 * t], refs[2 * t + 1])

    args = []
    for table, idx in sources:
        args += [table, _index_windows(idx)]
    return scatter(*args)


def _expert_kernel(te_ref, nu_ref, xs_ref, wg_ref, wu_ref, wd_ref, y_ref):
    del te_ref
    i = pl.program_id(0)

    @pl.when(i < nu_ref[0])
    def _():
        y_ref[...] = _swiglu_chunks(xs_ref[...].astype(BF16), wg_ref, wu_ref, wd_ref)

    @pl.when(i >= nu_ref[0])
    def _():
        y_ref[...] = jnp.zeros_like(y_ref)


def _grouped_experts(xs, tile_expert, n_used, w_gate, w_up, w_down):
    n_rows, d = xs.shape
    tm = TM_EXPERT
    ff = w_gate.shape[-1]
    grid_spec = pltpu.PrefetchScalarGridSpec(
        num_scalar_prefetch=2,
        grid=(n_rows // tm,),
        in_specs=[pl.BlockSpec((tm, d), lambda i, te, nu: (i, 0)),
                  pl.BlockSpec((None, d, ff), lambda i, te, nu: (te[i], 0, 0)),
                  pl.BlockSpec((None, d, ff), lambda i, te, nu: (te[i], 0, 0)),
                  pl.BlockSpec((None, ff, d), lambda i, te, nu: (te[i], 0, 0))],
        out_specs=pl.BlockSpec((tm, d), lambda i, te, nu: (i, 0)),
    )
    return pl.pallas_call(
        _expert_kernel,
        grid_spec=grid_spec,
        out_shape=jax.ShapeDtypeStruct((n_rows, d), F32),
        compiler_params=pltpu.CompilerParams(dimension_semantics=("arbitrary",),
                                             vmem_limit_bytes=60 * 1024 * 1024),
        name="moe_experts",
    )(tile_expert, n_used, xs, w_gate, w_up, w_down)


def _combine_kernel(rows0_ref, rows1_ref, x_ref, mod_ref, info_ref, fg_ref, o_ref, *, final):
    info = info_ref[...]
    ff = info[:, R_W0:R_W0 + 1] * rows0_ref[...] + info[:, R_W1:R_W1 + 1] * rows1_ref[...]
    y = x_ref[...] + mod_ref[...][5:6] * ff
    o_ref[...] = _final_norm(y, fg_ref[...]) if final else y


def _combine(y_sorted, pos, x, mod, info, final_gain, final):
    b, s, d = x.shape
    tm = min(TM_COMBINE, s)
    nt = s // tm
    rows = _gather_rows(y_sorted, pos.reshape(b * s, TOP_K).T.reshape(-1))
    second = (b * s) // tm
    return pl.pallas_call(
        functools.partial(_combine_kernel, final=final),
        grid=(b, nt),
        in_specs=[pl.BlockSpec((tm, d), lambda bi, i: (bi * nt + i, 0)),
                  pl.BlockSpec((tm, d), lambda bi, i: (second + bi * nt + i, 0)),
                  pl.BlockSpec((None, tm, d), lambda bi, i: (bi, i, 0)),
                  pl.BlockSpec((None, N_MOD, d), lambda bi, i: (bi, 0, 0)),
                  pl.BlockSpec((tm, LANES), lambda bi, i: (bi * nt + i, 0)),
                  _resident((1, d))],
        out_specs=pl.BlockSpec((None, tm, d), lambda bi, i: (bi, i, 0)),
        out_shape=jax.ShapeDtypeStruct(x.shape, F32),
        compiler_params=_params("parallel", "parallel"),
        name="moe_combine",
    )(rows, rows, x, mod, info, final_gain.reshape(1, d))


def _moe_ffn(x, mod, router_w, router_b, w_gate, w_up, w_down, final_gain, final):
    b, s, d = x.shape
    n = b * s
    n_e = router_w.shape[-1]
    tm = TM_EXPERT
    hp, info, cnt = _router(x, mod, router_w, router_b)

    counts = cnt[0, :n_e].astype(jnp.int32)
    tiles = (counts + tm - 1) // tm
    tile_end = jnp.cumsum(tiles)
    row_start = (tile_end - tiles) * tm
    n_tiles = (TOP_K * n) // tm + n_e
    tile_expert = jnp.minimum(
        jnp.sum(jnp.arange(n_tiles, dtype=jnp.int32)[:, None] >= tile_end[None, :], axis=1),
        n_e - 1).astype(jnp.int32)
    n_used = tile_end[-1:].astype(jnp.int32)
    experts = info[:, R_E0:R_E1 + 1].astype(jnp.int32)
    ranks = info[:, R_RANK0:R_RANK1 + 1].astype(jnp.int32)
    pos2 = row_start[experts] + ranks
    pos = pos2.reshape(-1)
    n_fill = n_e * tm
    fill = tiles * tm - counts
    fill_end = jnp.cumsum(fill)
    j = jnp.arange(n_fill, dtype=jnp.int32)
    group = jnp.sum(j[:, None] >= fill_end[None, :], axis=1)
    first = jnp.concatenate([row_start + counts, tile_end[-1:] * tm])[group]
    before = jnp.concatenate([jnp.zeros((1,), jnp.int32), fill_end])[group]
    fill_pos = first + j - before

    xs = _scatter_rows([(hp, pos2[:, k]) for k in range(TOP_K)]
                       + [(jnp.zeros((n_fill, d), hp.dtype), fill_pos)], n_tiles * tm)
    y_sorted = _grouped_experts(xs, tile_expert, n_used, w_gate, w_up, w_down)
    return _combine(y_sorted, pos, x, mod, info, final_gain, final)


def _rope_tables(seq_len):
    pos = jnp.arange(seq_len, dtype=jnp.int32)
    axis_pos = jnp.stack([pos // GRID_W, pos % GRID_W], axis=1).astype(F32)
    n_freq = HEAD_DIM // 4
    inv_freq = jnp.exp(-math.log(ROPE_THETA) * (2.0 * jnp.arange(n_freq, dtype=F32) / (HEAD_DIM // 2)))
    ang = axis_pos[:, :, None] * inv_freq[None, None, :]
    cos = jnp.cos(ang)
    sin = jnp.sin(ang)
    cos_t = jnp.concatenate([cos, cos], axis=-1).reshape(seq_len, HEAD_DIM)
    sin_t = jnp.concatenate([-sin, sin], axis=-1).reshape(seq_len, HEAD_DIM)
    return cos_t, sin_t


def kernel(x, c, w_mod, b_mod, w_in, q_norm_gain, k_norm_gain, conv_w, conv_b, lru_w_a, lru_b_a,
           lru_w_x, lru_b_x, lru_lambda, w_o_attn, w_o_lru, w_out, ffn_w_gate, ffn_w_up,
           ffn_w_down, router_w, router_b, moe_w_gate, moe_w_up, moe_w_down, final_gain):
    b, s, d = x.shape
    depth = w_in.shape[0]
    cos, sin = _rope_tables(s)
    mods = _modulation(c, w_mod, b_mod).reshape(depth, b, N_MOD, d)
    for l in range(depth):
        mod = mods[l]
        q, k, v, xr, gxg, sg = _inproj(x, mod, w_in[l].astype(BF16), q_norm_gain[l], k_norm_gain[l],
                                       cos, sin)
        attn0, attn1, h_fwd, h_bwd = _mixer(q, k, v, xr, conv_w[l], conv_b[l], lru_w_a[l], lru_b_a[l],
                                            lru_w_x[l], lru_b_x[l], lru_lambda[l], q_norm_gain[l])
        x = _merge(attn0, attn1, h_fwd, h_bwd, gxg, sg, x, mod, w_o_attn[l].astype(BF16),
                   w_o_lru[l].astype(BF16), w_out[l].astype(BF16))
        final = l == depth - 1
        j = l // 2
        if l % 2 == 0:
            x = _dense_ffn(x, mod, ffn_w_gate[j].astype(BF16), ffn_w_up[j].astype(BF16),
                           ffn_w_down[j].astype(BF16), final_gain, final)
        else:
            x = _moe_ffn(x, mod, router_w[j], router_b[j], moe_w_gate[j].astype(BF16),
                         moe_w_up[j].astype(BF16), moe_w_down[j].astype(BF16), final_gain, final)
    return x
```

```python
import functools
import math

import jax
import jax.numpy as jnp
from jax import lax
from jax.experimental import pallas as pl
from jax.experimental.pallas import tpu as pltpu
from jax.experimental.pallas import tpu_sc as plsc

HEAD_DIM = 128
N_Q_HEADS = 8
N_KV_HEADS = 2
Q_GROUP = N_Q_HEADS // N_KV_HEADS
GRID_W = 64
ROPE_THETA = 10000.0
ROPE_PAIR_SHIFT = HEAD_DIM // 4
LRU_BLOCKS = 8
LRU_C = 8.0
CONV_WIDTH = 4
N_MOD = 6
TOP_K = 2
NORM_EPS = 1e-6
LOG2_E = math.log2(math.e)
SOFTMAX_SAFE_LOG2_SHIFT = 55.0
BF16_ROUNDING_MARGIN = 1.0 + 2.0 ** -7

LANES = 128
SUBLANES = 8
VMEM_LIMIT_BYTES = 56 * 1024 * 1024

TM_INPROJ = 256
TQ_ATTN = 256
TM_MERGE = 512
TM_FFN = 512
TM_ROUTER = 256
TM_EXPERT = 512
TM_COMBINE = 256
COMBINE_PARTS = 2
FF_CHUNK = 512
KV_CHUNK = 512

BF16 = jnp.bfloat16
F32 = jnp.float32


def _params(*semantics):
    return pltpu.CompilerParams(dimension_semantics=semantics, vmem_limit_bytes=VMEM_LIMIT_BYTES)


def _resident(shape):
    zeros = (0,) * len(shape)
    return pl.BlockSpec(shape, lambda *_: zeros, pipeline_mode=pl.Buffered(1))


def _modulated_norm(x, shift, scale):
    ms = jnp.mean(x * x, axis=-1, keepdims=True)
    return x * lax.rsqrt(ms + NORM_EPS) * (1.0 + scale) + shift


def _mod_kernel(c_ref, w_ref, b_ref, o_ref):
    c = c_ref[...]
    act = c * jax.nn.sigmoid(c)
    o_ref[...] = jnp.dot(act, w_ref[...], preferred_element_type=F32,
                         precision=lax.Precision.HIGHEST) + b_ref[...]


def _modulation(c, w_mod, b_mod):
    depth, d, n = w_mod.shape
    b = c.shape[0]
    tn = 1536
    return pl.pallas_call(
        _mod_kernel,
        grid=(depth, n // tn),
        in_specs=[pl.BlockSpec((b, d), lambda l, j: (0, 0)),
                  pl.BlockSpec((None, d, tn), lambda l, j: (l, 0, j)),
                  pl.BlockSpec((None, 1, tn), lambda l, j: (l, 0, j))],
        out_specs=pl.BlockSpec((None, b, tn), lambda l, j: (l, 0, j)),
        out_shape=jax.ShapeDtypeStruct((depth, b, n), F32),
        compiler_params=_params("parallel", "parallel"),
        name="modulation",
    )(c, w_mod, b_mod.reshape(depth, 1, n))


def _inproj_kernel(x_ref, mod_ref, w_ref, qg_ref, kg_ref, cos_ref, sin_ref,
                   q_ref, k_ref, v_ref, xr_ref, gxg_ref, sg_ref, *, d_model):
    attn_w = N_Q_HEADS * HEAD_DIM
    kv_w = N_KV_HEADS * HEAD_DIM
    mod = mod_ref[...]
    hb = _modulated_norm(x_ref[...], mod[0:1], mod[1:2]).astype(BF16)

    cos = cos_ref[...]
    sin = sin_ref[...]
    lane = lax.broadcasted_iota(jnp.int32, cos.shape, 1)
    pair_first = (lane & ROPE_PAIR_SHIFT) == 0

    def norm_rope(y, gain, post_scale):
        r = lax.rsqrt(jnp.mean(y * y, axis=-1, keepdims=True) + NORM_EPS)
        yn = y * r * gain
        partner = jnp.where(pair_first,
                            pltpu.roll(yn, HEAD_DIM - ROPE_PAIR_SHIFT, 1),
                            pltpu.roll(yn, ROPE_PAIR_SHIFT, 1))
        return (yn * cos + partner * sin) * post_scale

    def proj(lo, width):
        return jnp.dot(hb, w_ref[:, lo:lo + width], preferred_element_type=F32)

    qf = proj(0, attn_w)
    qg = qg_ref[...]
    for h in range(N_Q_HEADS):
        sl = slice(h * HEAD_DIM, (h + 1) * HEAD_DIM)
        q_ref[:, sl] = norm_rope(qf[:, sl], qg, HEAD_DIM ** -0.5 * LOG2_E).astype(BF16)
    kf = proj(attn_w, kv_w)
    kg = kg_ref[...]
    for h in range(N_KV_HEADS):
        sl = slice(h * HEAD_DIM, (h + 1) * HEAD_DIM)
        k_ref[:, sl] = norm_rope(kf[:, sl], kg, 1.0).astype(BF16)
    v_ref[...] = proj(attn_w + kv_w, kv_w).astype(BF16)
    lo = attn_w + 2 * kv_w
    xr_ref[...] = proj(lo, d_model)
    gxg_ref[...] = jax.nn.gelu(proj(lo + d_model, d_model), approximate=True).astype(BF16)
    sg_ref[...] = jax.nn.sigmoid(proj(lo + 2 * d_model, 2 * d_model)).astype(BF16)


def _inproj(x, mod, w_in, q_gain, k_gain, cos, sin):
    b, s, d = x.shape
    tm = min(TM_INPROJ, s)
    attn_w = N_Q_HEADS * HEAD_DIM
    kv_w = N_KV_HEADS * HEAD_DIM
    row = lambda w: pl.BlockSpec((None, tm, w), lambda bi, i: (bi, i, 0))
    tab = pl.BlockSpec((tm, HEAD_DIM), lambda bi, i: (i, 0))
    outs = [(attn_w, BF16), (kv_w, BF16), (kv_w, BF16), (d, F32), (d, BF16), (2 * d, BF16)]
    return pl.pallas_call(
        functools.partial(_inproj_kernel, d_model=d),
        grid=(b, s // tm),
        in_specs=[row(d),
                  pl.BlockSpec((None, N_MOD, d), lambda bi, i: (bi, 0, 0)),
                  _resident(w_in.shape), _resident((1, HEAD_DIM)), _resident((1, HEAD_DIM)),
                  tab, tab],
        out_specs=[row(w) for w, _ in outs],
        out_shape=[jax.ShapeDtypeStruct((b, s, w), dt) for w, dt in outs],
        compiler_params=_params("parallel", "parallel"),
        name="inproj",
    )(x, mod, w_in, q_gain.reshape(1, HEAD_DIM), k_gain.reshape(1, HEAD_DIM), cos, sin)


def _lru_stage(xr_ref, prev_ref, next_ref, xs_scr, *, ts, tile, n_tiles):
    xs_scr[0:SUBLANES, :] = jnp.where(tile > 0, prev_ref[...], 0.0)
    xs_scr[SUBLANES:SUBLANES + ts, :] = xr_ref[...]
    xs_scr[SUBLANES + ts:, :] = jnp.where(tile < n_tiles - 1, next_ref[...], 0.0)


def _lru_block(n, cw_ref, cb_ref, wcat_ref, ba_ref, bx_ref, lam_ref, h_ref, xs_scr, carry_scr,
               *, ts, reverse):
    bw = LANES
    sl = slice(n * bw, (n + 1) * bw)

    def tap(shift):
        return xs_scr[SUBLANES + shift:SUBLANES + shift + ts, sl]

    cw = cw_ref[:, sl]
    xb = (cw[0:1] * tap(-2) + cw[1:2] * tap(-1) + cw[2:3] * tap(0) + cw[3:4] * tap(1)
          + cb_ref[:, sl])
    lam = lam_ref[:, sl]
    neg_c_softplus = -LRU_C * (jnp.maximum(-lam, 0.0) + jnp.log1p(jnp.exp(-jnp.abs(lam))))
    g = jnp.dot(xb.astype(BF16), wcat_ref[n], preferred_element_type=F32)
    r = jax.nn.sigmoid(g[:, :bw] + ba_ref[:, sl])
    i = jax.nn.sigmoid(g[:, bw:] + bx_ref[:, sl])
    log_a = neg_c_softplus * r
    a_all = jnp.exp(log_a)
    th = jnp.tanh(log_a)
    u_all = jnp.exp(0.5 * jnp.log(-2.0 * th / (1.0 - th))) * (i * xb)

    row = lax.broadcasted_iota(jnp.int32, (SUBLANES, bw), 0)
    n_groups = ts // SUBLANES
    state = carry_scr[0:1, sl]
    states = [None] * n_groups
    for v in (range(n_groups - 1, -1, -1) if reverse else range(n_groups)):
        rows = slice(v * SUBLANES, (v + 1) * SUBLANES)
        a = a_all[rows]
        u = u_all[rows]
        for d in (1, 2, 4):
            keep = (row < SUBLANES - d) if reverse else (row >= d)
            shift = (SUBLANES - d) if reverse else d
            a_prev = jnp.where(keep, pltpu.roll(a, shift, 0), 1.0)
            u_prev = jnp.where(keep, pltpu.roll(u, shift, 0), 0.0)
            u = a * u_prev + u
            a = a * a_prev
        h = u + a * state
        states[v] = h
        state = h[0:1] if reverse else h[SUBLANES - 1:SUBLANES]
    carry_scr[0:1, sl] = state
    h_ref[:, sl] = jnp.concatenate(states, axis=0).astype(BF16)


def _mixer_kernel(q_ref, k_ref, v_ref, xr_ref, prev_ref, next_ref, cw_ref, cb_ref, wcat_ref, ba_ref,
                  bx_ref, lam_ref, qg_ref, o_ref, h_ref, vext_scr, shift_scr, xs_scr, carry_scr,
                  *, tq, s_len, kv_chunk, n_tiles, reverse):
    nt = (((1,), (1,)), ((), ()))
    i = pl.program_id(1)

    @pl.when(i == 0)
    def _():
        carry_scr[...] = jnp.zeros_like(carry_scr)
        vext_scr[:, :HEAD_DIM] = v_ref[...]
        vext_scr[:, HEAD_DIM:] = jnp.ones((s_len, HEAD_DIM), BF16)
        kf = k_ref[...].astype(F32)
        k_norm_max = jnp.max(jnp.sqrt(jnp.sum(kf * kf, axis=-1, keepdims=True)))
        q_norm_max = jnp.max(jnp.abs(qg_ref[...])) * (LOG2_E * BF16_ROUNDING_MARGIN)
        shift_scr[0] = q_norm_max * k_norm_max

    lru_stage = functools.partial(_lru_stage, xr_ref, prev_ref, next_ref, xs_scr, ts=tq,
                                  tile=(n_tiles - 1 - i) if reverse else i, n_tiles=n_tiles)
    lru_block = functools.partial(_lru_block, cw_ref=cw_ref, cb_ref=cb_ref, wcat_ref=wcat_ref,
                                  ba_ref=ba_ref, bx_ref=bx_ref, lam_ref=lam_ref, h_ref=h_ref,
                                  xs_scr=xs_scr, carry_scr=carry_scr, ts=tq, reverse=reverse)

    q = q_ref[...]
    qs = jnp.concatenate([q[:, h * HEAD_DIM:(h + 1) * HEAD_DIM] for h in range(Q_GROUP)], axis=0)
    shift = shift_scr[0]
    safe = shift <= SOFTMAX_SAFE_LOG2_SHIFT

    def finish(acc):
        o = acc[:, :HEAD_DIM] / acc[:, HEAD_DIM:]
        for h in range(Q_GROUP):
            o_ref[:, h * HEAD_DIM:(h + 1) * HEAD_DIM] = o[h * tq:(h + 1) * tq].astype(BF16)

    @pl.when(safe)
    def _():
        lru_stage()
        chunks = list(range(0, s_len, kv_chunk))
        per_chunk = -(-LRU_BLOCKS // len(chunks))
        acc = None
        for c, lo in enumerate(chunks):
            s = lax.dot_general(qs, k_ref[lo:lo + kv_chunk, :], nt, preferred_element_type=F32)
            p = jnp.exp2(s - shift).astype(BF16)
            pv = jnp.dot(p, vext_scr[lo:lo + kv_chunk, :], preferred_element_type=F32)
            acc = pv if acc is None else acc + pv
            for n in range(c * per_chunk, min((c + 1) * per_chunk, LRU_BLOCKS)):
                lru_block(n)
        finish(acc)

    @pl.when(jnp.logical_not(safe))
    def _():
        s = lax.dot_general(qs, k_ref[...], nt, preferred_element_type=F32)
        p = jnp.exp2(s - jnp.max(s, axis=-1, keepdims=True)).astype(BF16)
        finish(jnp.dot(p, vext_scr[...], preferred_element_type=F32))
        lru_stage()
        for n in range(LRU_BLOCKS):
            lru_block(n)


def _mixer_group(q, k, v, xr, conv_w, conv_b, wcat, b_a, b_x, lam, q_gain, group, reverse):
    b, s, w = xr.shape
    tq = min(TQ_ATTN, s)
    n_tiles = s // tq
    blocks8 = tq // SUBLANES
    gw = Q_GROUP * HEAD_DIM
    tile = (lambda i: n_tiles - 1 - i) if reverse else (lambda i: i)
    qspec = pl.BlockSpec((None, tq, gw), lambda bi, i: (bi, i, group))
    kvspec = pl.BlockSpec((None, s, HEAD_DIM), lambda bi, i: (bi, 0, group))
    main = pl.BlockSpec((None, tq, w), lambda bi, i: (bi, tile(i), 0))
    prev = pl.BlockSpec((None, SUBLANES, w),
                        lambda bi, i: (bi, jnp.maximum(tile(i) * blocks8 - 1, 0), 0))
    nxt = pl.BlockSpec((None, SUBLANES, w),
                       lambda bi, i: (bi, jnp.minimum((tile(i) + 1) * blocks8, s // SUBLANES - 1), 0))
    vec = _resident((1, w))
    return pl.pallas_call(
        functools.partial(_mixer_kernel, tq=tq, s_len=s, kv_chunk=min(KV_CHUNK, s), n_tiles=n_tiles,
                          reverse=reverse),
        grid=(b, n_tiles),
        in_specs=[qspec, kvspec, kvspec, main, prev, nxt, _resident((CONV_WIDTH, w)), vec,
                  _resident(wcat.shape), vec, vec, vec, _resident((1, HEAD_DIM))],
        out_specs=[pl.BlockSpec((None, tq, gw), lambda bi, i: (bi, i, 0)), main],
        out_shape=[jax.ShapeDtypeStruct((b, s, gw), BF16), jax.ShapeDtypeStruct((b, s, w), BF16)],
        scratch_shapes=[pltpu.VMEM((s, 2 * HEAD_DIM), BF16), pltpu.SMEM((1,), F32),
                        pltpu.VMEM((tq + 2 * SUBLANES, w), F32), pltpu.VMEM((SUBLANES, w), F32)],
        compiler_params=_params("parallel", "arbitrary"),
        name="mixer_bwd" if reverse else "mixer_fwd",
    )(q, k, v, xr, xr, xr, conv_w, conv_b.reshape(1, w), wcat, b_a.reshape(1, w), b_x.reshape(1, w),
      lam.reshape(1, w), q_gain.reshape(1, HEAD_DIM))


def _mixer(q, k, v, xr, conv_w, conv_b, w_a, b_a, w_x, b_x, lam, q_gain):
    wcat = jnp.concatenate([w_a, w_x], axis=-1).astype(BF16)
    outs = [_mixer_group(q, k, v, xr, conv_w, conv_b, wcat[g], b_a[g], b_x[g], lam[g], q_gain, g,
                         g == 1)
            for g in range(N_KV_HEADS)]
    (attn0, h_fwd), (attn1, h_bwd) = outs
    return attn0, attn1, h_fwd, h_bwd


def _merge_kernel(attn0_ref, attn1_ref, hf_ref, hb_ref, gxg_ref, sg_ref, x_ref, mod_ref, woa_ref,
                  wol_ref, wout_ref, o_ref, *, d_model):
    gw = attn0_ref.shape[-1]
    ya = (jnp.dot(attn0_ref[...], woa_ref[:gw, :], preferred_element_type=F32)
          + jnp.dot(attn1_ref[...], woa_ref[gw:, :], preferred_element_type=F32))
    ylru = (hf_ref[...].astype(F32) + hb_ref[...].astype(F32)) * gxg_ref[...].astype(F32)
    yl = jnp.dot(ylru.astype(BF16), wol_ref[...], preferred_element_type=F32)
    sg = sg_ref[...].astype(F32)
    merged = sg[:, :d_model] * ya + sg[:, d_model:] * yl
    out = jnp.dot(merged.astype(BF16), wout_ref[...], preferred_element_type=F32)
    o_ref[...] = x_ref[...] + mod_ref[...][2:3] * out


def _merge(attn0, attn1, h_fwd, h_bwd, gxg, sg, x, mod, w_o_attn, w_o_lru, w_out):
    b, s, d = x.shape
    tm = min(TM_MERGE, s)
    row = lambda w: pl.BlockSpec((None, tm, w), lambda bi, i: (bi, i, 0))
    gw = attn0.shape[-1]
    return pl.pallas_call(
        functools.partial(_merge_kernel, d_model=d),
        grid=(b, s // tm),
        in_specs=[row(gw), row(gw), row(d), row(d), row(d), row(2 * d), row(d),
                  pl.BlockSpec((None, N_MOD, d), lambda bi, i: (bi, 0, 0)),
                  _resident(w_o_attn.shape), _resident(w_o_lru.shape), _resident(w_out.shape)],
        out_specs=row(d),
        out_shape=jax.ShapeDtypeStruct(x.shape, F32),
        compiler_params=_params("parallel", "parallel"),
        name="merge",
    )(attn0, attn1, h_fwd, h_bwd, gxg, sg, x, mod, w_o_attn, w_o_lru, w_out)


def _swiglu_chunks(hb, wg_ref, wu_ref, wd_ref):
    d_ff = wg_ref.shape[-1]
    acc = None
    for lo in range(0, d_ff, FF_CHUNK):
        hi = min(lo + FF_CHUNK, d_ff)
        g = jnp.dot(hb, wg_ref[:, lo:hi], preferred_element_type=F32)
        u = jnp.dot(hb, wu_ref[:, lo:hi], preferred_element_type=F32)
        act = (g * jax.nn.sigmoid(g) * u).astype(BF16)
        y = jnp.dot(act, wd_ref[lo:hi, :], preferred_element_type=F32)
        acc = y if acc is None else acc + y
    return acc


def _final_norm(x, gain):
    return x * lax.rsqrt(jnp.mean(x * x, axis=-1, keepdims=True) + NORM_EPS) * gain


def _ffn_kernel(x_ref, mod_ref, wg_ref, wu_ref, wd_ref, fg_ref, o_ref, *, final):
    x = x_ref[...]
    mod = mod_ref[...]
    hb = _modulated_norm(x, mod[3:4], mod[4:5]).astype(BF16)
    y = x + mod[5:6] * _swiglu_chunks(hb, wg_ref, wu_ref, wd_ref)
    o_ref[...] = _final_norm(y, fg_ref[...]) if final else y


def _dense_ffn(x, mod, w_gate, w_up, w_down, final_gain, final):
    b, s, d = x.shape
    tm = min(TM_FFN, s)
    row = pl.BlockSpec((None, tm, d), lambda bi, i: (bi, i, 0))
    return pl.pallas_call(
        functools.partial(_ffn_kernel, final=final),
        grid=(b, s // tm),
        in_specs=[row, pl.BlockSpec((None, N_MOD, d), lambda bi, i: (bi, 0, 0)),
                  _resident(w_gate.shape), _resident(w_up.shape), _resident(w_down.shape),
                  _resident((1, d))],
        out_specs=row,
        out_shape=jax.ShapeDtypeStruct(x.shape, F32),
        compiler_params=_params("parallel", "parallel"),
        name="dense_ffn",
    )(x, mod, w_gate, w_up, w_down, final_gain.reshape(1, d))


R_E0, R_E1, R_RANK0, R_RANK1, R_W0, R_W1 = range(6)


def _router_kernel(x_ref, mod_ref, rw_ref, rb_ref, hp_ref, info_ref, cnt_ref, carry_scr, *, tm):
    @pl.when((pl.program_id(0) == 0) & (pl.program_id(1) == 0))
    def _():
        carry_scr[...] = jnp.zeros_like(carry_scr)

    mod = mod_ref[...]
    h = _modulated_norm(x_ref[...], mod[3:4], mod[4:5])
    hp_ref[...] = h

    w = rw_ref[...]
    h_hi = h.astype(BF16)
    w_hi = w.astype(BF16)
    h_lo = (h - h_hi.astype(F32)).astype(BF16)
    w_lo = (w - w_hi.astype(F32)).astype(BF16)
    logits = (jnp.dot(h_hi, w_hi, preferred_element_type=F32)
              + (jnp.dot(h_lo, w_hi, preferred_element_type=F32)
                 + jnp.dot(h_hi, w_lo, preferred_element_type=F32))) + rb_ref[...]
    lane = lax.broadcasted_iota(jnp.int32, logits.shape, 1)
    m1 = jnp.max(logits, axis=-1, keepdims=True)
    e0 = jnp.min(jnp.where(logits == m1, lane, LANES), axis=-1, keepdims=True)
    rest = jnp.where(lane == e0, -jnp.inf, logits)
    m2 = jnp.max(rest, axis=-1, keepdims=True)
    e1 = jnp.min(jnp.where(rest == m2, lane, LANES), axis=-1, keepdims=True)
    t = jnp.exp(m2 - m1)
    w0 = 1.0 / (1.0 + t)
    w1 = t * w0

    hit0 = lane == e0
    hit1 = lane == e1
    onehot = jnp.where(hit0 | hit1, 1.0, 0.0)
    ri = lax.broadcasted_iota(jnp.int32, (tm, tm), 0)
    ci = lax.broadcasted_iota(jnp.int32, (tm, tm), 1)
    before = jnp.where(ci < ri, 1.0, 0.0).astype(BF16)
    seen = jnp.dot(before, onehot.astype(BF16), preferred_element_type=F32) + carry_scr[0:1, :]
    rank0 = jnp.sum(jnp.where(hit0, seen, 0.0), axis=-1, keepdims=True)
    rank1 = jnp.sum(jnp.where(hit1, seen, 0.0), axis=-1, keepdims=True)
    carry = carry_scr[0:1, :] + jnp.sum(onehot, axis=0, keepdims=True)
    carry_scr[0:1, :] = carry
    cnt_ref[...] = jnp.broadcast_to(carry, cnt_ref.shape)

    info = jnp.zeros(logits.shape, F32)
    for col, val in ((R_E0, e0.astype(F32)), (R_E1, e1.astype(F32)), (R_RANK0, rank0),
                     (R_RANK1, rank1), (R_W0, w0), (R_W1, w1)):
        info = jnp.where(lane == col, val, info)
    info_ref[...] = info


def _router(x, mod, router_w, router_b):
    b, s, d = x.shape
    tm = min(TM_ROUTER, s)
    n_e = router_w.shape[-1]
    rw = jnp.zeros((d, LANES), F32).at[:, :n_e].set(router_w)
    rb = jnp.full((1, LANES), -jnp.inf, F32).at[0, :n_e].set(router_b)
    nt = s // tm
    return pl.pallas_call(
        functools.partial(_router_kernel, tm=tm),
        grid=(b, nt),
        in_specs=[pl.BlockSpec((None, tm, d), lambda bi, i: (bi, i, 0)),
                  pl.BlockSpec((None, N_MOD, d), lambda bi, i: (bi, 0, 0)),
                  _resident(rw.shape), _resident(rb.shape)],
        out_specs=[pl.BlockSpec((tm, d), lambda bi, i: (bi * nt + i, 0)),
                   pl.BlockSpec((tm, LANES), lambda bi, i: (bi * nt + i, 0)),
                   _resident((SUBLANES, LANES))],
        out_shape=[jax.ShapeDtypeStruct((b * s, d), F32),
                   jax.ShapeDtypeStruct((b * s, LANES), F32),
                   jax.ShapeDtypeStruct((SUBLANES, LANES), F32)],
        scratch_shapes=[pltpu.VMEM((SUBLANES, LANES), F32)],
        compiler_params=_params("arbitrary", "arbitrary"),
        name="router",
    )(x, mod, rw, rb)


SC_GATHER_WINDOW = 32


def _index_windows(idx):
    windows = idx.reshape(idx.shape[0] // SC_GATHER_WINDOW, SC_GATHER_WINDOW)
    return jnp.pad(windows, ((0, 0), (0, LANES - SC_GATHER_WINDOW)))


def _gather_rows(table, idx):
    m = idx.shape[0]
    d = table.shape[1]
    mesh = plsc.VectorSubcoreMesh(core_axis_name="core", subcore_axis_name="subcore")

    @functools.partial(pl.kernel, out_type=jax.ShapeDtypeStruct((m, d), table.dtype), mesh=mesh,
                       scratch_types=[])
    def gather(table_hbm, idx_hbm, out_hbm):
        def body(idx_vmem, out_vmem):
            pltpu.sync_copy(table_hbm.at[idx_vmem.at[0, pl.ds(0, SC_GATHER_WINDOW)]], out_vmem)

        pltpu.emit_pipeline(
            body,
            grid=(m // SC_GATHER_WINDOW,),
            in_specs=[pl.BlockSpec((1, LANES), index_map=lambda i: (i, 0))],
            out_specs=[pl.BlockSpec((SC_GATHER_WINDOW, d), index_map=lambda i: (i, 0))],
            core_axis_name=("core", "subcore"),
            dimension_semantics=(pltpu.PARALLEL,),
        )(idx_hbm, out_hbm)

    return gather(table, _index_windows(idx))


def _scatter_rows(sources, n_rows):
    d = sources[0][0].shape[1]
    mesh = plsc.VectorSubcoreMesh(core_axis_name="core", subcore_axis_name="subcore")

    @functools.partial(pl.kernel, out_type=jax.ShapeDtypeStruct((n_rows, d), sources[0][0].dtype),
                       mesh=mesh, scratch_types=[])
    def scatter(*refs):
        out_hbm = refs[-1]

        def body(rows_vmem, idx_vmem):
            pltpu.sync_copy(rows_vmem, out_hbm.at[idx_vmem.at[0, pl.ds(0, SC_GATHER_WINDOW)]])

        for t, (table, _) in enumerate(sources):
            pltpu.emit_pipeline(
                body,
                grid=(table.shape[0] // SC_GATHER_WINDOW,),
                in_specs=[pl.BlockSpec((SC_GATHER_WINDOW, d), index_map=lambda i: (i, 0)),
                          pl.BlockSpec((1, LANES), index_map=lambda i: (i, 0))],
                out_specs=[],
                core_axis_name=("core", "subcore"),
                dimension_semantics=(pltpu.PARALLEL,),
            )(refs[2 * t], refs[2 * t + 1])

    args = []
    for table, idx in sources:
        args += [table, _index_windows(idx)]
    return scatter(*args)


def _expert_kernel(te_ref, nu_ref, xs_ref, wg_ref, wu_ref, wd_ref, y_ref):
    del te_ref
    i = pl.program_id(0)

    @pl.when(i < nu_ref[0])
    def _():
        y_ref[...] = _swiglu_chunks(xs_ref[...].astype(BF16), wg_ref, wu_ref, wd_ref)

    @pl.when(i >= nu_ref[0])
    def _():
        y_ref[...] = jnp.zeros_like(y_ref)


def _grouped_experts(xs, tile_expert, n_used, w_gate, w_up, w_down):
    n_rows, d = xs.shape
    tm = TM_EXPERT
    ff = w_gate.shape[-1]
    grid_spec = pltpu.PrefetchScalarGridSpec(
        num_scalar_prefetch=2,
        grid=(n_rows // tm,),
        in_specs=[pl.BlockSpec((tm, d), lambda i, te, nu: (i, 0)),
                  pl.BlockSpec((None, d, ff), lambda i, te, nu: (te[i], 0, 0)),
                  pl.BlockSpec((None, d, ff), lambda i, te, nu: (te[i], 0, 0)),
                  pl.BlockSpec((None, ff, d), lambda i, te, nu: (te[i], 0, 0))],
        out_specs=pl.BlockSpec((tm, d), lambda i, te, nu: (i, 0)),
    )
    return pl.pallas_call(
        _expert_kernel,
        grid_spec=grid_spec,
        out_shape=jax.ShapeDtypeStruct((n_rows, d), F32),
        compiler_params=pltpu.CompilerParams(dimension_semantics=("arbitrary",),
                                             vmem_limit_bytes=60 * 1024 * 1024),
        name="moe_experts",
    )(tile_expert, n_used, xs, w_gate, w_up, w_down)


def _combine_kernel(rows0_ref, rows1_ref, x_ref, mod_ref, info_ref, fg_ref, *rest, final):
    o_ref = rest[-1]
    info = info_ref[...]
    ff = info[:, R_W0:R_W0 + 1] * rows0_ref[...] + info[:, R_W1:R_W1 + 1] * rows1_ref[...]
    y = x_ref[...] + mod_ref[...][5:6] * ff
    o_ref[...] = _final_norm(y, fg_ref[...]) if final else y


def _combine(y_sorted, pos, x, mod, info, final_gain, final):
    b, s, d = x.shape
    tm = min(TM_COMBINE, s)
    nt = s // tm
    parts = COMBINE_PARTS if b % COMBINE_PARTS == 0 else 1
    bp = b // parts
    pos2 = pos.reshape(b * s, TOP_K)
    second = (bp * s) // tm
    out = None
    for p in range(parts):
        b0 = p * bp
        rows = _gather_rows(y_sorted, pos2[b0 * s:(b0 + bp) * s].T.reshape(-1))
        in_specs = [pl.BlockSpec((tm, d), lambda bi, i: (bi * nt + i, 0)),
                    pl.BlockSpec((tm, d), lambda bi, i: (second + bi * nt + i, 0)),
                    pl.BlockSpec((None, tm, d), lambda bi, i, b0=b0: (b0 + bi, i, 0)),
                    pl.BlockSpec((None, N_MOD, d), lambda bi, i, b0=b0: (b0 + bi, 0, 0)),
                    pl.BlockSpec((tm, LANES), lambda bi, i, b0=b0: ((b0 + bi) * nt + i, 0)),
                    _resident((1, d))]
        args = [rows, rows, x, mod, info, final_gain.reshape(1, d)]
        if out is not None:
            in_specs.append(pl.BlockSpec(memory_space=pl.ANY))
            args.append(out)
        out = pl.pallas_call(
            functools.partial(_combine_kernel, final=final),
            grid=(bp, nt),
            in_specs=in_specs,
            out_specs=pl.BlockSpec((None, tm, d), lambda bi, i, b0=b0: (b0 + bi, i, 0)),
            out_shape=jax.ShapeDtypeStruct(x.shape, F32),
            input_output_aliases={} if p == 0 else {len(args) - 1: 0},
            compiler_params=_params("parallel", "parallel"),
            name="moe_combine",
        )(*args)
    return out


def _moe_ffn(x, mod, router_w, router_b, w_gate, w_up, w_down, final_gain, final):
    b, s, d = x.shape
    n = b * s
    n_e = router_w.shape[-1]
    tm = TM_EXPERT
    hp, info, cnt = _router(x, mod, router_w, router_b)

    counts = cnt[0, :n_e].astype(jnp.int32)
    tiles = (counts + tm - 1) // tm
    tile_end = jnp.cumsum(tiles)
    row_start = (tile_end - tiles) * tm
    n_tiles = (TOP_K * n) // tm + n_e
    tile_expert = jnp.minimum(
        jnp.sum(jnp.arange(n_tiles, dtype=jnp.int32)[:, None] >= tile_end[None, :], axis=1),
        n_e - 1).astype(jnp.int32)
    n_used = tile_end[-1:].astype(jnp.int32)
    experts = info[:, R_E0:R_E1 + 1].astype(jnp.int32)
    ranks = info[:, R_RANK0:R_RANK1 + 1].astype(jnp.int32)
    pos2 = row_start[experts] + ranks
    pos = pos2.reshape(-1)
    n_fill = n_e * tm
    fill = tiles * tm - counts
    fill_end = jnp.cumsum(fill)
    j = jnp.arange(n_fill, dtype=jnp.int32)
    group = jnp.sum(j[:, None] >= fill_end[None, :], axis=1)
    first = jnp.concatenate([row_start + counts, tile_end[-1:] * tm])[group]
    before = jnp.concatenate([jnp.zeros((1,), jnp.int32), fill_end])[group]
    fill_pos = first + j - before

    xs = _scatter_rows([(hp, pos2[:, k]) for k in range(TOP_K)]
                       + [(jnp.zeros((n_fill, d), hp.dtype), fill_pos)], n_tiles * tm)
    y_sorted = _grouped_experts(xs, tile_expert, n_used, w_gate, w_up, w_down)
    return _combine(y_sorted, pos, x, mod, info, final_gain, final)


def _rope_tables(seq_len):
    pos = jnp.arange(seq_len, dtype=jnp.int32)
    axis_pos = jnp.stack([pos // GRID_W, pos % GRID_W], axis=1).astype(F32)
    n_freq = HEAD_DIM // 4
    inv_freq = jnp.exp(-math.log(ROPE_THETA) * (2.0 * jnp.arange(n_freq, dtype=F32) / (HEAD_DIM // 2)))
    ang = axis_pos[:, :, None] * inv_freq[None, None, :]
    cos = jnp.cos(ang)
    sin = jnp.sin(ang)
    cos_t = jnp.concatenate([cos, cos], axis=-1).reshape(seq_len, HEAD_DIM)
    sin_t = jnp.concatenate([-sin, sin], axis=-1).reshape(seq_len, HEAD_DIM)
    return cos_t, sin_t


def kernel(x, c, w_mod, b_mod, w_in, q_norm_gain, k_norm_gain, conv_w, conv_b, lru_w_a, lru_b_a,
           lru_w_x, lru_b_x, lru_lambda, w_o_attn, w_o_lru, w_out, ffn_w_gate, ffn_w_up,
           ffn_w_down, router_w, router_b, moe_w_gate, moe_w_up, moe_w_down, final_gain):
    b, s, d = x.shape
    depth = w_in.shape[0]
    cos, sin = _rope_tables(s)
    mods = _modulation(c, w_mod, b_mod).reshape(depth, b, N_MOD, d)
    for l in range(depth):
        mod = mods[l]
        q, k, v, xr, gxg, sg = _inproj(x, mod, w_in[l].astype(BF16), q_norm_gain[l], k_norm_gain[l],
                                       cos, sin)
        attn0, attn1, h_fwd, h_bwd = _mixer(q, k, v, xr, conv_w[l], conv_b[l], lru_w_a[l], lru_b_a[l],
                                            lru_w_x[l], lru_b_x[l], lru_lambda[l], q_norm_gain[l])
        x = _merge(attn0, attn1, h_fwd, h_bwd, gxg, sg, x, mod, w_o_attn[l].astype(BF16),
                   w_o_lru[l].astype(BF16), w_out[l].astype(BF16))
        final = l == depth - 1
        j = l // 2
        if l % 2 == 0:
            x = _dense_ffn(x, mod, ffn_w_gate[j].astype(BF16), ffn_w_up[j].astype(BF16),
                           ffn_w_down[j].astype(BF16), final_gain, final)
        else:
            x = _moe_ffn(x, mod, router_w[j], router_b[j], moe_w_gate[j].astype(BF16),
                         moe_w_up[j].astype(BF16), moe_w_down[j].astype(BF16), final_gain, final)
    return x
```

```python
import functools
import math

import jax
import jax.numpy as jnp
from jax import lax
from jax.experimental import pallas as pl
from jax.experimental.pallas import tpu as pltpu
from jax.experimental.pallas import tpu_sc as plsc

HEAD_DIM = 128
N_Q_HEADS = 8
N_KV_HEADS = 2
Q_GROUP = N_Q_HEADS // N_KV_HEADS
GRID_W = 64
ROPE_THETA = 10000.0
ROPE_PAIR_SHIFT = HEAD_DIM // 4
LRU_BLOCKS = 8
LRU_C = 8.0
CONV_WIDTH = 4
N_MOD = 6
TOP_K = 2
NORM_EPS = 1e-6
LOG2_E = math.log2(math.e)
SOFTMAX_SAFE_LOG2_SHIFT = 55.0
BF16_ROUNDING_MARGIN = 1.0 + 2.0 ** -7

LANES = 128
SUBLANES = 8
VMEM_LIMIT_BYTES = 56 * 1024 * 1024

TM_INPROJ = 256
TQ_ATTN = 256
TM_MERGE = 512
TM_FFN = 512
TM_ROUTER = 256
TM_EXPERT = 512
TM_COMBINE = 256
COMBINE_PARTS = 4
FF_CHUNK = 512
KV_CHUNK = 512

BF16 = jnp.bfloat16
F32 = jnp.float32


def _params(*semantics):
    return pltpu.CompilerParams(dimension_semantics=semantics, vmem_limit_bytes=VMEM_LIMIT_BYTES)


def _resident(shape):
    zeros = (0,) * len(shape)
    return pl.BlockSpec(shape, lambda *_: zeros, pipeline_mode=pl.Buffered(1))


def _modulated_norm(x, shift, scale):
    ms = jnp.mean(x * x, axis=-1, keepdims=True)
    return x * lax.rsqrt(ms + NORM_EPS) * (1.0 + scale) + shift


def _mod_kernel(c_ref, w_ref, b_ref, o_ref):
    c = c_ref[...]
    act = c * jax.nn.sigmoid(c)
    o_ref[...] = jnp.dot(act, w_ref[...], preferred_element_type=F32,
                         precision=lax.Precision.HIGHEST) + b_ref[...]


def _modulation(c, w_mod, b_mod):
    depth, d, n = w_mod.shape
    b = c.shape[0]
    tn = 1536
    return pl.pallas_call(
        _mod_kernel,
        grid=(depth, n // tn),
        in_specs=[pl.BlockSpec((b, d), lambda l, j: (0, 0)),
                  pl.BlockSpec((None, d, tn), lambda l, j: (l, 0, j)),
                  pl.BlockSpec((None, 1, tn), lambda l, j: (l, 0, j))],
        out_specs=pl.BlockSpec((None, b, tn), lambda l, j: (l, 0, j)),
        out_shape=jax.ShapeDtypeStruct((depth, b, n), F32),
        compiler_params=_params("parallel", "parallel"),
        name="modulation",
    )(c, w_mod, b_mod.reshape(depth, 1, n))


def _inproj_kernel(x_ref, mod_ref, w_ref, qg_ref, kg_ref, cos_ref, sin_ref,
                   q_ref, k_ref, v_ref, xr_ref, gxg_ref, sg_ref, *, d_model):
    attn_w = N_Q_HEADS * HEAD_DIM
    kv_w = N_KV_HEADS * HEAD_DIM
    mod = mod_ref[...]
    hb = _modulated_norm(x_ref[...], mod[0:1], mod[1:2]).astype(BF16)

    cos = cos_ref[...]
    sin = sin_ref[...]
    lane = lax.broadcasted_iota(jnp.int32, cos.shape, 1)
    pair_first = (lane & ROPE_PAIR_SHIFT) == 0

    def norm_rope(y, gain, post_scale):
        r = lax.rsqrt(jnp.mean(y * y, axis=-1, keepdims=True) + NORM_EPS)
        yn = y * r * gain
        partner = jnp.where(pair_first,
                            pltpu.roll(yn, HEAD_DIM - ROPE_PAIR_SHIFT, 1),
                            pltpu.roll(yn, ROPE_PAIR_SHIFT, 1))
        return (yn * cos + partner * sin) * post_scale

    def proj(lo, width):
        return jnp.dot(hb, w_ref[:, lo:lo + width], preferred_element_type=F32)

    qf = proj(0, attn_w)
    qg = qg_ref[...]
    for h in range(N_Q_HEADS):
        sl = slice(h * HEAD_DIM, (h + 1) * HEAD_DIM)
        q_ref[:, sl] = norm_rope(qf[:, sl], qg, HEAD_DIM ** -0.5 * LOG2_E).astype(BF16)
    kf = proj(attn_w, kv_w)
    kg = kg_ref[...]
    for h in range(N_KV_HEADS):
        sl = slice(h * HEAD_DIM, (h + 1) * HEAD_DIM)
        k_ref[:, sl] = norm_rope(kf[:, sl], kg, 1.0).astype(BF16)
    v_ref[...] = proj(attn_w + kv_w, kv_w).astype(BF16)
    lo = attn_w + 2 * kv_w
    xr_ref[...] = proj(lo, d_model)
    gxg_ref[...] = jax.nn.gelu(proj(lo + d_model, d_model), approximate=True).astype(BF16)
    sg_ref[...] = jax.nn.sigmoid(proj(lo + 2 * d_model, 2 * d_model)).astype(BF16)


def _inproj(x, mod, w_in, q_gain, k_gain, cos, sin):
    b, s, d = x.shape
    tm = min(TM_INPROJ, s)
    attn_w = N_Q_HEADS * HEAD_DIM
    kv_w = N_KV_HEADS * HEAD_DIM
    row = lambda w: pl.BlockSpec((None, tm, w), lambda bi, i: (bi, i, 0))
    tab = pl.BlockSpec((tm, HEAD_DIM), lambda bi, i: (i, 0))
    outs = [(attn_w, BF16), (kv_w, BF16), (kv_w, BF16), (d, F32), (d, BF16), (2 * d, BF16)]
    return pl.pallas_call(
        functools.partial(_inproj_kernel, d_model=d),
        grid=(b, s // tm),
        in_specs=[row(d),
                  pl.BlockSpec((None, N_MOD, d), lambda bi, i: (bi, 0, 0)),
                  _resident(w_in.shape), _resident((1, HEAD_DIM)), _resident((1, HEAD_DIM)),
                  tab, tab],
        out_specs=[row(w) for w, _ in outs],
        out_shape=[jax.ShapeDtypeStruct((b, s, w), dt) for w, dt in outs],
        compiler_params=_params("parallel", "parallel"),
        name="inproj",
    )(x, mod, w_in, q_gain.reshape(1, HEAD_DIM), k_gain.reshape(1, HEAD_DIM), cos, sin)


def _lru_stage(xr_ref, prev_ref, next_ref, xs_scr, *, ts, tile, n_tiles):
    xs_scr[0:SUBLANES, :] = jnp.where(tile > 0, prev_ref[...], 0.0)
    xs_scr[SUBLANES:SUBLANES + ts, :] = xr_ref[...]
    xs_scr[SUBLANES + ts:, :] = jnp.where(tile < n_tiles - 1, next_ref[...], 0.0)


def _lru_block(n, cw_ref, cb_ref, wcat_ref, ba_ref, bx_ref, lam_ref, h_ref, xs_scr, carry_scr,
               *, ts, reverse):
    bw = LANES
    sl = slice(n * bw, (n + 1) * bw)

    def tap(shift):
        return xs_scr[SUBLANES + shift:SUBLANES + shift + ts, sl]

    cw = cw_ref[:, sl]
    xb = (cw[0:1] * tap(-2) + cw[1:2] * tap(-1) + cw[2:3] * tap(0) + cw[3:4] * tap(1)
          + cb_ref[:, sl])
    lam = lam_ref[:, sl]
    neg_c_softplus = -LRU_C * (jnp.maximum(-lam, 0.0) + jnp.log1p(jnp.exp(-jnp.abs(lam))))
    g = jnp.dot(xb.astype(BF16), wcat_ref[n], preferred_element_type=F32)
    r = jax.nn.sigmoid(g[:, :bw] + ba_ref[:, sl])
    i = jax.nn.sigmoid(g[:, bw:] + bx_ref[:, sl])
    log_a = neg_c_softplus * r
    a_all = jnp.exp(log_a)
    th = jnp.tanh(log_a)
    u_all = jnp.exp(0.5 * jnp.log(-2.0 * th / (1.0 - th))) * (i * xb)

    row = lax.broadcasted_iota(jnp.int32, (SUBLANES, bw), 0)
    n_groups = ts // SUBLANES
    state = carry_scr[0:1, sl]
    states = [None] * n_groups
    for v in (range(n_groups - 1, -1, -1) if reverse else range(n_groups)):
        rows = slice(v * SUBLANES, (v + 1) * SUBLANES)
        a = a_all[rows]
        u = u_all[rows]
        for d in (1, 2, 4):
            keep = (row < SUBLANES - d) if reverse else (row >= d)
            shift = (SUBLANES - d) if reverse else d
            a_prev = jnp.where(keep, pltpu.roll(a, shift, 0), 1.0)
            u_prev = jnp.where(keep, pltpu.roll(u, shift, 0), 0.0)
            u = a * u_prev + u
            a = a * a_prev
        h = u + a * state
        states[v] = h
        state = h[0:1] if reverse else h[SUBLANES - 1:SUBLANES]
    carry_scr[0:1, sl] = state
    h_ref[:, sl] = jnp.concatenate(states, axis=0).astype(BF16)


def _mixer_kernel(q_ref, k_ref, v_ref, xr_ref, prev_ref, next_ref, cw_ref, cb_ref, wcat_ref, ba_ref,
                  bx_ref, lam_ref, qg_ref, o_ref, h_ref, vext_scr, shift_scr, xs_scr, carry_scr,
                  *, tq, s_len, kv_chunk, n_tiles, reverse):
    nt = (((1,), (1,)), ((), ()))
    i = pl.program_id(1)

    @pl.when(i == 0)
    def _():
        carry_scr[...] = jnp.zeros_like(carry_scr)
        vext_scr[:, :HEAD_DIM] = v_ref[...]
        vext_scr[:, HEAD_DIM:] = jnp.ones((s_len, HEAD_DIM), BF16)
        kf = k_ref[...].astype(F32)
        k_norm_max = jnp.max(jnp.sqrt(jnp.sum(kf * kf, axis=-1, keepdims=True)))
        q_norm_max = jnp.max(jnp.abs(qg_ref[...])) * (LOG2_E * BF16_ROUNDING_MARGIN)
        shift_scr[0] = q_norm_max * k_norm_max

    lru_stage = functools.partial(_lru_stage, xr_ref, prev_ref, next_ref, xs_scr, ts=tq,
                                  tile=(n_tiles - 1 - i) if reverse else i, n_tiles=n_tiles)
    lru_block = functools.partial(_lru_block, cw_ref=cw_ref, cb_ref=cb_ref, wcat_ref=wcat_ref,
                                  ba_ref=ba_ref, bx_ref=bx_ref, lam_ref=lam_ref, h_ref=h_ref,
                                  xs_scr=xs_scr, carry_scr=carry_scr, ts=tq, reverse=reverse)

    q = q_ref[...]
    qs = jnp.concatenate([q[:, h * HEAD_DIM:(h + 1) * HEAD_DIM] for h in range(Q_GROUP)], axis=0)
    shift = shift_scr[0]
    safe = shift <= SOFTMAX_SAFE_LOG2_SHIFT

    def finish(acc):
        o = acc[:, :HEAD_DIM] / acc[:, HEAD_DIM:]
        for h in range(Q_GROUP):
            o_ref[:, h * HEAD_DIM:(h + 1) * HEAD_DIM] = o[h * tq:(h + 1) * tq].astype(BF16)

    @pl.when(safe)
    def _():
        lru_stage()
        chunks = list(range(0, s_len, kv_chunk))
        per_chunk = -(-LRU_BLOCKS // len(chunks))
        acc = None
        for c, lo in enumerate(chunks):
            s = lax.dot_general(qs, k_ref[lo:lo + kv_chunk, :], nt, preferred_element_type=F32)
            p = jnp.exp2(s - shift).astype(BF16)
            pv = jnp.dot(p, vext_scr[lo:lo + kv_chunk, :], preferred_element_type=F32)
            acc = pv if acc is None else acc + pv
            for n in range(c * per_chunk, min((c + 1) * per_chunk, LRU_BLOCKS)):
                lru_block(n)
        finish(acc)

    @pl.when(jnp.logical_not(safe))
    def _():
        s = lax.dot_general(qs, k_ref[...], nt, preferred_element_type=F32)
        p = jnp.exp2(s - jnp.max(s, axis=-1, keepdims=True)).astype(BF16)
        finish(jnp.dot(p, vext_scr[...], preferred_element_type=F32))
        lru_stage()
        for n in range(LRU_BLOCKS):
            lru_block(n)


def _mixer_group(q, k, v, xr, conv_w, conv_b, wcat, b_a, b_x, lam, q_gain, group, reverse):
    b, s, w = xr.shape
    tq = min(TQ_ATTN, s)
    n_tiles = s // tq
    blocks8 = tq // SUBLANES
    gw = Q_GROUP * HEAD_DIM
    tile = (lambda i: n_tiles - 1 - i) if reverse else (lambda i: i)
    qspec = pl.BlockSpec((None, tq, gw), lambda bi, i: (bi, i, group))
    kvspec = pl.BlockSpec((None, s, HEAD_DIM), lambda bi, i: (bi, 0, group))
    main = pl.BlockSpec((None, tq, w), lambda bi, i: (bi, tile(i), 0))
    prev = pl.BlockSpec((None, SUBLANES, w),
                        lambda bi, i: (bi, jnp.maximum(tile(i) * blocks8 - 1, 0), 0))
    nxt = pl.BlockSpec((None, SUBLANES, w),
                       lambda bi, i: (bi, jnp.minimum((tile(i) + 1) * blocks8, s // SUBLANES - 1), 0))
    vec = _resident((1, w))
    return pl.pallas_call(
        functools.partial(_mixer_kernel, tq=tq, s_len=s, kv_chunk=min(KV_CHUNK, s), n_tiles=n_tiles,
                          reverse=reverse),
        grid=(b, n_tiles),
        in_specs=[qspec, kvspec, kvspec, main, prev, nxt, _resident((CONV_WIDTH, w)), vec,
                  _resident(wcat.shape), vec, vec, vec, _resident((1, HEAD_DIM))],
        out_specs=[pl.BlockSpec((None, tq, gw), lambda bi, i: (bi, i, 0)), main],
        out_shape=[jax.ShapeDtypeStruct((b, s, gw), BF16), jax.ShapeDtypeStruct((b, s, w), BF16)],
        scratch_shapes=[pltpu.VMEM((s, 2 * HEAD_DIM), BF16), pltpu.SMEM((1,), F32),
                        pltpu.VMEM((tq + 2 * SUBLANES, w), F32), pltpu.VMEM((SUBLANES, w), F32)],
        compiler_params=_params("parallel", "arbitrary"),
        name="mixer_bwd" if reverse else "mixer_fwd",
    )(q, k, v, xr, xr, xr, conv_w, conv_b.reshape(1, w), wcat, b_a.reshape(1, w), b_x.reshape(1, w),
      lam.reshape(1, w), q_gain.reshape(1, HEAD_DIM))


def _mixer(q, k, v, xr, conv_w, conv_b, w_a, b_a, w_x, b_x, lam, q_gain):
    wcat = jnp.concatenate([w_a, w_x], axis=-1).astype(BF16)
    outs = [_mixer_group(q, k, v, xr, conv_w, conv_b, wcat[g], b_a[g], b_x[g], lam[g], q_gain, g,
                         g == 1)
            for g in range(N_KV_HEADS)]
    (attn0, h_fwd), (attn1, h_bwd) = outs
    return attn0, attn1, h_fwd, h_bwd


def _merge_kernel(attn0_ref, attn1_ref, hf_ref, hb_ref, gxg_ref, sg_ref, x_ref, mod_ref, woa_ref,
                  wol_ref, wout_ref, o_ref, *, d_model):
    gw = attn0_ref.shape[-1]
    ya = (jnp.dot(attn0_ref[...], woa_ref[:gw, :], preferred_element_type=F32)
          + jnp.dot(attn1_ref[...], woa_ref[gw:, :], preferred_element_type=F32))
    ylru = (hf_ref[...].astype(F32) + hb_ref[...].astype(F32)) * gxg_ref[...].astype(F32)
    yl = jnp.dot(ylru.astype(BF16), wol_ref[...], preferred_element_type=F32)
    sg = sg_ref[...].astype(F32)
    merged = sg[:, :d_model] * ya + sg[:, d_model:] * yl
    out = jnp.dot(merged.astype(BF16), wout_ref[...], preferred_element_type=F32)
    o_ref[...] = x_ref[...] + mod_ref[...][2:3] * out


def _merge(attn0, attn1, h_fwd, h_bwd, gxg, sg, x, mod, w_o_attn, w_o_lru, w_out):
    b, s, d = x.shape
    tm = min(TM_MERGE, s)
    row = lambda w: pl.BlockSpec((None, tm, w), lambda bi, i: (bi, i, 0))
    gw = attn0.shape[-1]
    return pl.pallas_call(
        functools.partial(_merge_kernel, d_model=d),
        grid=(b, s // tm),
        in_specs=[row(gw), row(gw), row(d), row(d), row(d), row(2 * d), row(d),
                  pl.BlockSpec((None, N_MOD, d), lambda bi, i: (bi, 0, 0)),
                  _resident(w_o_attn.shape), _resident(w_o_lru.shape), _resident(w_out.shape)],
        out_specs=row(d),
        out_shape=jax.ShapeDtypeStruct(x.shape, F32),
        compiler_params=_params("parallel", "parallel"),
        name="merge",
    )(attn0, attn1, h_fwd, h_bwd, gxg, sg, x, mod, w_o_attn, w_o_lru, w_out)


def _swiglu_chunks(hb, wg_ref, wu_ref, wd_ref):
    d_ff = wg_ref.shape[-1]
    acc = None
    for lo in range(0, d_ff, FF_CHUNK):
        hi = min(lo + FF_CHUNK, d_ff)
        g = jnp.dot(hb, wg_ref[:, lo:hi], preferred_element_type=F32)
        u = jnp.dot(hb, wu_ref[:, lo:hi], preferred_element_type=F32)
        act = (g * jax.nn.sigmoid(g) * u).astype(BF16)
        y = jnp.dot(act, wd_ref[lo:hi, :], preferred_element_type=F32)
        acc = y if acc is None else acc + y
    return acc


def _final_norm(x, gain):
    return x * lax.rsqrt(jnp.mean(x * x, axis=-1, keepdims=True) + NORM_EPS) * gain


def _ffn_kernel(x_ref, mod_ref, wg_ref, wu_ref, wd_ref, fg_ref, o_ref, *, final):
    x = x_ref[...]
    mod = mod_ref[...]
    hb = _modulated_norm(x, mod[3:4], mod[4:5]).astype(BF16)
    y = x + mod[5:6] * _swiglu_chunks(hb, wg_ref, wu_ref, wd_ref)
    o_ref[...] = _final_norm(y, fg_ref[...]) if final else y


def _dense_ffn(x, mod, w_gate, w_up, w_down, final_gain, final):
    b, s, d = x.shape
    tm = min(TM_FFN, s)
    row = pl.BlockSpec((None, tm, d), lambda bi, i: (bi, i, 0))
    return pl.pallas_call(
        functools.partial(_ffn_kernel, final=final),
        grid=(b, s // tm),
        in_specs=[row, pl.BlockSpec((None, N_MOD, d), lambda bi, i: (bi, 0, 0)),
                  _resident(w_gate.shape), _resident(w_up.shape), _resident(w_down.shape),
                  _resident((1, d))],
        out_specs=row,
        out_shape=jax.ShapeDtypeStruct(x.shape, F32),
        compiler_params=_params("parallel", "parallel"),
        name="dense_ffn",
    )(x, mod, w_gate, w_up, w_down, final_gain.reshape(1, d))


R_E0, R_E1, R_RANK0, R_RANK1, R_W0, R_W1 = range(6)


def _router_kernel(x_ref, mod_ref, rw_ref, rb_ref, hp_ref, info_ref, cnt_ref, carry_scr, *, tm):
    @pl.when((pl.program_id(0) == 0) & (pl.program_id(1) == 0))
    def _():
        carry_scr[...] = jnp.zeros_like(carry_scr)

    mod = mod_ref[...]
    h = _modulated_norm(x_ref[...], mod[3:4], mod[4:5])
    hp_ref[...] = h

    w = rw_ref[...]
    h_hi = h.astype(BF16)
    w_hi = w.astype(BF16)
    h_lo = (h - h_hi.astype(F32)).astype(BF16)
    w_lo = (w - w_hi.astype(F32)).astype(BF16)
    logits = (jnp.dot(h_hi, w_hi, preferred_element_type=F32)
              + (jnp.dot(h_lo, w_hi, preferred_element_type=F32)
                 + jnp.dot(h_hi, w_lo, preferred_element_type=F32))) + rb_ref[...]
    lane = lax.broadcasted_iota(jnp.int32, logits.shape, 1)
    m1 = jnp.max(logits, axis=-1, keepdims=True)
    e0 = jnp.min(jnp.where(logits == m1, lane, LANES), axis=-1, keepdims=True)
    rest = jnp.where(lane == e0, -jnp.inf, logits)
    m2 = jnp.max(rest, axis=-1, keepdims=True)
    e1 = jnp.min(jnp.where(rest == m2, lane, LANES), axis=-1, keepdims=True)
    t = jnp.exp(m2 - m1)
    w0 = 1.0 / (1.0 + t)
    w1 = t * w0

    hit0 = lane == e0
    hit1 = lane == e1
    onehot = jnp.where(hit0 | hit1, 1.0, 0.0)
    ri = lax.broadcasted_iota(jnp.int32, (tm, tm), 0)
    ci = lax.broadcasted_iota(jnp.int32, (tm, tm), 1)
    before = jnp.where(ci < ri, 1.0, 0.0).astype(BF16)
    seen = jnp.dot(before, onehot.astype(BF16), preferred_element_type=F32) + carry_scr[0:1, :]
    rank0 = jnp.sum(jnp.where(hit0, seen, 0.0), axis=-1, keepdims=True)
    rank1 = jnp.sum(jnp.where(hit1, seen, 0.0), axis=-1, keepdims=True)
    carry = carry_scr[0:1, :] + jnp.sum(onehot, axis=0, keepdims=True)
    carry_scr[0:1, :] = carry
    cnt_ref[...] = jnp.broadcast_to(carry, cnt_ref.shape)

    info = jnp.zeros(logits.shape, F32)
    for col, val in ((R_E0, e0.astype(F32)), (R_E1, e1.astype(F32)), (R_RANK0, rank0),
                     (R_RANK1, rank1), (R_W0, w0), (R_W1, w1)):
        info = jnp.where(lane == col, val, info)
    info_ref[...] = info


def _router(x, mod, router_w, router_b):
    b, s, d = x.shape
    tm = min(TM_ROUTER, s)
    n_e = router_w.shape[-1]
    rw = jnp.zeros((d, LANES), F32).at[:, :n_e].set(router_w)
    rb = jnp.full((1, LANES), -jnp.inf, F32).at[0, :n_e].set(router_b)
    nt = s // tm
    return pl.pallas_call(
        functools.partial(_router_kernel, tm=tm),
        grid=(b, nt),
        in_specs=[pl.BlockSpec((None, tm, d), lambda bi, i: (bi, i, 0)),
                  pl.BlockSpec((None, N_MOD, d), lambda bi, i: (bi, 0, 0)),
                  _resident(rw.shape), _resident(rb.shape)],
        out_specs=[pl.BlockSpec((tm, d), lambda bi, i: (bi * nt + i, 0)),
                   pl.BlockSpec((tm, LANES), lambda bi, i: (bi * nt + i, 0)),
                   _resident((SUBLANES, LANES))],
        out_shape=[jax.ShapeDtypeStruct((b * s, d), F32),
                   jax.ShapeDtypeStruct((b * s, LANES), F32),
                   jax.ShapeDtypeStruct((SUBLANES, LANES), F32)],
        scratch_shapes=[pltpu.VMEM((SUBLANES, LANES), F32)],
        compiler_params=_params("arbitrary", "arbitrary"),
        name="router",
    )(x, mod, rw, rb)


SC_GATHER_WINDOW = 32


def _index_windows(idx):
    windows = idx.reshape(idx.shape[0] // SC_GATHER_WINDOW, SC_GATHER_WINDOW)
    return jnp.pad(windows, ((0, 0), (0, LANES - SC_GATHER_WINDOW)))


def _gather_rows(table, idx):
    m = idx.shape[0]
    d = table.shape[1]
    mesh = plsc.VectorSubcoreMesh(core_axis_name="core", subcore_axis_name="subcore")

    @functools.partial(pl.kernel, out_type=jax.ShapeDtypeStruct((m, d), table.dtype), mesh=mesh,
                       scratch_types=[])
    def gather(table_hbm, idx_hbm, out_hbm):
        def body(idx_vmem, out_vmem):
            pltpu.sync_copy(table_hbm.at[idx_vmem.at[0, pl.ds(0, SC_GATHER_WINDOW)]], out_vmem)

        pltpu.emit_pipeline(
            body,
            grid=(m // SC_GATHER_WINDOW,),
            in_specs=[pl.BlockSpec((1, LANES), index_map=lambda i: (i, 0))],
            out_specs=[pl.BlockSpec((SC_GATHER_WINDOW, d), index_map=lambda i: (i, 0))],
            core_axis_name=("core", "subcore"),
            dimension_semantics=(pltpu.PARALLEL,),
        )(idx_hbm, out_hbm)

    return gather(table, _index_windows(idx))


def _scatter_rows(sources, n_rows):
    d = sources[0][0].shape[1]
    mesh = plsc.VectorSubcoreMesh(core_axis_name="core", subcore_axis_name="subcore")

    @functools.partial(pl.kernel, out_type=jax.ShapeDtypeStruct((n_rows, d), sources[0][0].dtype),
                       mesh=mesh, scratch_types=[])
    def scatter(*refs):
        out_hbm = refs[-1]

        def body(rows_vmem, idx_vmem):
            pltpu.sync_copy(rows_vmem, out_hbm.at[idx_vmem.at[0, pl.ds(0, SC_GATHER_WINDOW)]])

        for t, (table, _) in enumerate(sources):
            pltpu.emit_pipeline(
                body,
                grid=(table.shape[0] // SC_GATHER_WINDOW,),
                in_specs=[pl.BlockSpec((SC_GATHER_WINDOW, d), index_map=lambda i: (i, 0)),
                          pl.BlockSpec((1, LANES), index_map=lambda i: (i, 0))],
                out_specs=[],
                core_axis_name=("core", "subcore"),
                dimension_semantics=(pltpu.PARALLEL,),
            )(refs[2 * t], refs[2 * t + 1])

    args = []
    for table, idx in sources:
        args += [table, _index_windows(idx)]
    return scatter(*args)


def _expert_kernel(te_ref, nu_ref, xs_ref, wg_ref, wu_ref, wd_ref, y_ref):
    del te_ref
    i = pl.program_id(0)

    @pl.when(i < nu_ref[0])
    def _():
        y_ref[...] = _swiglu_chunks(xs_ref[...].astype(BF16), wg_ref, wu_ref, wd_ref)

    @pl.when(i >= nu_ref[0])
    def _():
        y_ref[...] = jnp.zeros_like(y_ref)


def _grouped_experts(xs, tile_expert, n_used, w_gate, w_up, w_down):
    n_rows, d = xs.shape
    tm = TM_EXPERT
    ff = w_gate.shape[-1]
    grid_spec = pltpu.PrefetchScalarGridSpec(
        num_scalar_prefetch=2,
        grid=(n_rows // tm,),
        in_specs=[pl.BlockSpec((tm, d), lambda i, te, nu: (i, 0)),
                  pl.BlockSpec((None, d, ff), lambda i, te, nu: (te[i], 0, 0)),
                  pl.BlockSpec((None, d, ff), lambda i, te, nu: (te[i], 0, 0)),
                  pl.BlockSpec((None, ff, d), lambda i, te, nu: (te[i], 0, 0))],
        out_specs=pl.BlockSpec((tm, d), lambda i, te, nu: (i, 0)),
    )
    return pl.pallas_call(
        _expert_kernel,
        grid_spec=grid_spec,
        out_shape=jax.ShapeDtypeStruct((n_rows, d), F32),
        compiler_params=pltpu.CompilerParams(dimension_semantics=("arbitrary",),
                                             vmem_limit_bytes=60 * 1024 * 1024),
        name="moe_experts",
    )(tile_expert, n_used, xs, w_gate, w_up, w_down)


def _combine_kernel(rows0_ref, rows1_ref, x_ref, mod_ref, info_ref, fg_ref, *rest, final):
    o_ref = rest[-1]
    info = info_ref[...]
    ff = info[:, R_W0:R_W0 + 1] * rows0_ref[...] + info[:, R_W1:R_W1 + 1] * rows1_ref[...]
    y = x_ref[...] + mod_ref[...][5:6] * ff
    o_ref[...] = _final_norm(y, fg_ref[...]) if final else y


def _combine(y_sorted, pos, x, mod, info, final_gain, final):
    b, s, d = x.shape
    tm = min(TM_COMBINE, s)
    nt = s // tm
    parts = COMBINE_PARTS if b % COMBINE_PARTS == 0 else 1
    bp = b // parts
    pos2 = pos.reshape(b * s, TOP_K)
    second = (bp * s) // tm
    out = None
    for p in range(parts):
        b0 = p * bp
        rows = _gather_rows(y_sorted, pos2[b0 * s:(b0 + bp) * s].T.reshape(-1))
        in_specs = [pl.BlockSpec((tm, d), lambda bi, i: (bi * nt + i, 0)),
                    pl.BlockSpec((tm, d), lambda bi, i: (second + bi * nt + i, 0)),
                    pl.BlockSpec((None, tm, d), lambda bi, i, b0=b0: (b0 + bi, i, 0)),
                    pl.BlockSpec((None, N_MOD, d), lambda bi, i, b0=b0: (b0 + bi, 0, 0)),
                    pl.BlockSpec((tm, LANES), lambda bi, i, b0=b0: ((b0 + bi) * nt + i, 0)),
                    _resident((1, d))]
        args = [rows, rows, x, mod, info, final_gain.reshape(1, d)]
        if out is not None:
            in_specs.append(pl.BlockSpec(memory_space=pl.ANY))
            args.append(out)
        out = pl.pallas_call(
            functools.partial(_combine_kernel, final=final),
            grid=(bp, nt),
            in_specs=in_specs,
            out_specs=pl.BlockSpec((None, tm, d), lambda bi, i, b0=b0: (b0 + bi, i, 0)),
            out_shape=jax.ShapeDtypeStruct(x.shape, F32),
            input_output_aliases={} if p == 0 else {len(args) - 1: 0},
            compiler_params=_params("parallel", "parallel"),
            name="moe_combine",
        )(*args)
    return out


def _moe_ffn(x, mod, router_w, router_b, w_gate, w_up, w_down, final_gain, final):
    b, s, d = x.shape
    n = b * s
    n_e = router_w.shape[-1]
    tm = TM_EXPERT
    hp, info, cnt = _router(x, mod, router_w, router_b)

    counts = cnt[0, :n_e].astype(jnp.int32)
    tiles = (counts + tm - 1) // tm
    tile_end = jnp.cumsum(tiles)
    row_start = (tile_end - tiles) * tm
    n_tiles = (TOP_K * n) // tm + n_e
    tile_expert = jnp.minimum(
        jnp.sum(jnp.arange(n_tiles, dtype=jnp.int32)[:, None] >= tile_end[None, :], axis=1),
        n_e - 1).astype(jnp.int32)
    n_used = tile_end[-1:].astype(jnp.int32)
    experts = info[:, R_E0:R_E1 + 1].astype(jnp.int32)
    ranks = info[:, R_RANK0:R_RANK1 + 1].astype(jnp.int32)
    pos2 = row_start[experts] + ranks
    pos = pos2.reshape(-1)
    n_fill = n_e * tm
    fill = tiles * tm - counts
    fill_end = jnp.cumsum(fill)
    j = jnp.arange(n_fill, dtype=jnp.int32)
    group = jnp.sum(j[:, None] >= fill_end[None, :], axis=1)
    first = jnp.concatenate([row_start + counts, tile_end[-1:] * tm])[group]
    before = jnp.concatenate([jnp.zeros((1,), jnp.int32), fill_end])[group]
    fill_pos = first + j - before

    xs = _scatter_rows([(hp, pos2[:, k]) for k in range(TOP_K)]
                       + [(jnp.zeros((n_fill, d), hp.dtype), fill_pos)], n_tiles * tm)
    y_sorted = _grouped_experts(xs, tile_expert, n_used, w_gate, w_up, w_down)
    return _combine(y_sorted, pos, x, mod, info, final_gain, final)


def _rope_tables(seq_len):
    pos = jnp.arange(seq_len, dtype=jnp.int32)
    axis_pos = jnp.stack([pos // GRID_W, pos % GRID_W], axis=1).astype(F32)
    n_freq = HEAD_DIM // 4
    inv_freq = jnp.exp(-math.log(ROPE_THETA) * (2.0 * jnp.arange(n_freq, dtype=F32) / (HEAD_DIM // 2)))
    ang = axis_pos[:, :, None] * inv_freq[None, None, :]
    cos = jnp.cos(ang)
    sin = jnp.sin(ang)
    cos_t = jnp.concatenate([cos, cos], axis=-1).reshape(seq_len, HEAD_DIM)
    sin_t = jnp.concatenate([-sin, sin], axis=-1).reshape(seq_len, HEAD_DIM)
    return cos_t, sin_t


def kernel(x, c, w_mod, b_mod, w_in, q_norm_gain, k_norm_gain, conv_w, conv_b, lru_w_a, lru_b_a,
           lru_w_x, lru_b_x, lru_lambda, w_o_attn, w_o_lru, w_out, ffn_w_gate, ffn_w_up,
           ffn_w_down, router_w, router_b, moe_w_gate, moe_w_up, moe_w_down, final_gain):
    b, s, d = x.shape
    depth = w_in.shape[0]
    cos, sin = _rope_tables(s)
    mods = _modulation(c, w_mod, b_mod).reshape(depth, b, N_MOD, d)
    for l in range(depth):
        mod = mods[l]
        q, k, v, xr, gxg, sg = _inproj(x, mod, w_in[l].astype(BF16), q_norm_gain[l], k_norm_gain[l],
                                       cos, sin)
        attn0, attn1, h_fwd, h_bwd = _mixer(q, k, v, xr, conv_w[l], conv_b[l], lru_w_a[l], lru_b_a[l],
                                            lru_w_x[l], lru_b_x[l], lru_lambda[l], q_norm_gain[l])
        x = _merge(attn0, attn1, h_fwd, h_bwd, gxg, sg, x, mod, w_o_attn[l].astype(BF16),
                   w_o_lru[l].astype(BF16), w_out[l].astype(BF16))
        final = l == depth - 1
        j = l // 2
        if l % 2 == 0:
            x = _dense_ffn(x, mod, ffn_w_gate[j].astype(BF16), ffn_w_up[j].astype(BF16),
                           ffn_w_down[j].astype(BF16), final_gain, final)
        else:
            x = _moe_ffn(x, mod, router_w[j], router_b[j], moe_w_gate[j].astype(BF16),
                         moe_w_up[j].astype(BF16), moe_w_down[j].astype(BF16), final_gain, final)
    return x
```
